```python
import jax, jax.numpy as jnp
from jax import lax
import numpy as np

D_MODEL = 2048
BATCH = 4
SEQ = 2048
DEPTH = 1
DEC_BATCH = 128
DEC_SEQ = 8
PAST_LEN = 16384
PAGE_SIZE = 128

D_MIX = D_MODEL
D_A = D_MIX // 2
A_HEADS = 8
A_EXPAND = 128
A_VDIM = D_A // A_HEADS
FORGET_DIM = A_HEADS * A_EXPAND
C_CONV = D_MIX - D_A
CONV_W = 31
CHUNK = 64
D_FF = 5632
MEM_LEN = 256
XA_HEADS = 4
XA_HDIM = 128
XA_DIM = XA_HEADS * XA_HDIM
D_IN = 2 * FORGET_DIM + 2 * D_A + 2 * C_CONV
EPS = 1e-6
FFN_RES = 0.5

kernel_name = 'hymba_hgrn2_conformer_macaron_step'


def rmsnorm(x, g):
    xf = x.astype(jnp.float32)
    y = xf * lax.rsqrt(jnp.mean(xf * xf, axis=-1, keepdims=True) + EPS)
    return (y * g.astype(jnp.float32)).astype(x.dtype)


def swiglu_ffn(x, wg, wu, wd):
    return (jax.nn.silu(x @ wg) * (x @ wu)) @ wd


def hgrn2_recurrence(q, k, logf, v, s0):
    N, L, H, E = q.shape
    DV = v.shape[-1]
    C = CHUNK if L % CHUNK == 0 else L
    nc = L // C

    def to_chunks(a):
        return a.reshape(N, nc, C, H, a.shape[-1]).transpose(1, 0, 3, 2, 4)

    causal = jnp.tril(jnp.ones((C, C), dtype=bool))[None, None, :, :, None]

    def step(S, inp):
        qc, kc, fc, vc = inp
        b = jnp.cumsum(fc, axis=2)
        decay = jnp.exp(jnp.where(causal, b[:, :, :, None, :] - b[:, :, None, :, :], -jnp.inf))
        A = jnp.einsum('nhte,nhtse,nhse->nhts', qc, decay, kc)
        o = jnp.einsum('nhts,nhsd->nhtd', A, vc) + jnp.einsum('nhte,nhed->nhtd', qc * jnp.exp(b), S)
        b_end = b[:, :, -1, :]
        k_to_end = kc * jnp.exp(b_end[:, :, None, :] - b)
        S = jnp.exp(b_end)[..., None] * S + jnp.einsum('nhse,nhsd->nhed', k_to_end, vc)
        return S, o

    S, o = lax.scan(step, s0, (to_chunks(q), to_chunks(k), to_chunks(logf), to_chunks(v)))
    o = o.transpose(1, 0, 3, 2, 4).reshape(N, L, H, DV)
    return o, S


def mixer(h, s0, buf0, lb, w_in, gnorm, conv_w, conv_b, ln_g, ln_b, w_out):
    N, L, _ = h.shape
    f32 = jnp.float32
    z = h @ w_in
    idx = [FORGET_DIM, 2 * FORGET_DIM, 2 * FORGET_DIM + D_A, 2 * FORGET_DIM + 2 * D_A,
           2 * FORGET_DIM + 2 * D_A + C_CONV]
    zq, zf, zi, zg, za, zb = jnp.split(z, idx, axis=-1)
    q = jax.nn.silu(zq.astype(f32)).reshape(N, L, A_HEADS, A_EXPAND)
    f = lb + (1.0 - lb) * jax.nn.sigmoid(zf.astype(f32))
    logf = jnp.log(f).reshape(N, L, A_HEADS, A_EXPAND)
    k = (1.0 - f).reshape(N, L, A_HEADS, A_EXPAND)
    v = zi.astype(f32).reshape(N, L, A_HEADS, A_VDIM)
    o, s_new = hgrn2_recurrence(q, k, logf, v, s0.astype(f32))
    o = rmsnorm(o, gnorm) * jax.nn.silu(zg.astype(f32).reshape(N, L, A_HEADS, A_VDIM))
    o_a = o.reshape(N, L, D_A)
    u = za.astype(f32) * jax.nn.sigmoid(zb.astype(f32))
    ucat = jnp.concatenate([buf0.astype(f32), u], axis=1)
    buf_new = ucat[:, ucat.shape[1] - (CONV_W - 1):]
    dw = lax.conv_general_dilated(ucat, conv_w.astype(f32)[:, None, :], (1,), 'VALID',
                                  dimension_numbers=('NWC', 'WIO', 'NWC'),
                                  feature_group_count=C_CONV) + conv_b.astype(f32)
    mu = jnp.mean(dw, axis=-1, keepdims=True)
    var = jnp.mean(jnp.square(dw - mu), axis=-1, keepdims=True)
    ln = (dw - mu) * lax.rsqrt(var + EPS) * ln_g.astype(f32) + ln_b.astype(f32)
    o_b = jax.nn.silu(ln)
    out = jnp.concatenate([o_a, o_b], axis=-1).astype(h.dtype) @ w_out
    return out, s_new.astype(s0.dtype), buf_new.astype(buf0.dtype)


def mem_kv(mem, g, wk, wv):
    N, M, _ = mem.shape
    mn = rmsnorm(mem, g)
    return (mn @ wk).reshape(N, M, XA_HEADS, XA_HDIM), (mn @ wv).reshape(N, M, XA_HEADS, XA_HDIM)


def cross_attn(h, mk, mv, wq, wo):
    N, L, _ = h.shape
    q = (h @ wq).reshape(N, L, XA_HEADS, XA_HDIM)
    s = jnp.einsum('nlhd,nmhd->nhlm', q.astype(jnp.float32), mk.astype(jnp.float32)) * (XA_HDIM ** -0.5)
    p = jax.nn.softmax(s, axis=-1)
    o = jnp.einsum('nhlm,nmhd->nlhd', p, mv.astype(jnp.float32)).reshape(N, L, XA_DIM)
    return o.astype(h.dtype) @ wo


def block(x, mk, mv, s0, buf0, lb, p):
    (n_f1, f1_g, f1_u, f1_d, n_mix, w_in, gnorm, conv_w, conv_b, ln_g, ln_b, w_out,
     n_xa, xa_q, xa_o, n_f2, f2_g, f2_u, f2_d) = p
    x = x + FFN_RES * swiglu_ffn(rmsnorm(x, n_f1), f1_g, f1_u, f1_d)
    m, s_new, buf_new = mixer(rmsnorm(x, n_mix), s0, buf0, lb, w_in, gnorm, conv_w, conv_b, ln_g, ln_b, w_out)
    x = x + m
    x = x + cross_attn(rmsnorm(x, n_xa), mk, mv, xa_q, xa_o)
    x = x + FFN_RES * swiglu_ffn(rmsnorm(x, n_f2), f2_g, f2_u, f2_d)
    return x, s_new, buf_new


def setup_inputs(seed: int = 0) -> dict:
    key = jax.random.key(seed)
    ks = iter(jax.random.split(key, 48))

    def nrm(shape, scale):
        return scale * jax.random.normal(next(ks), shape, jnp.float32)

    def gain(shape):
        return 1.0 + nrm(shape, 0.05)

    return {
        'x_prompt': nrm((BATCH, SEQ, D_MODEL), 1.0),
        'x_sample': nrm((DEC_BATCH, DEC_SEQ, D_MODEL), 1.0),
        'mem_prompt': nrm((BATCH, MEM_LEN, D_MODEL), 1.0),
        'state_hgrn': nrm((DEPTH, DEC_BATCH, A_HEADS, A_EXPAND, A_VDIM), 0.1),
        'state_conv': nrm((DEPTH, DEC_BATCH, CONV_W - 1, C_CONV), 0.5),
        'cache_mem_k': nrm((DEPTH, DEC_BATCH, MEM_LEN, XA_HEADS, XA_HDIM), 1.0),
        'cache_mem_v': nrm((DEPTH, DEC_BATCH, MEM_LEN, XA_HEADS, XA_HDIM), 1.0),
        'norm_ffn1': gain((DEPTH, D_MODEL)),
        'ffn1_w_gate': nrm((DEPTH, D_MODEL, D_FF), D_MODEL ** -0.5),
        'ffn1_w_up': nrm((DEPTH, D_MODEL, D_FF), D_MODEL ** -0.5),
        'ffn1_w_down': nrm((DEPTH, D_FF, D_MODEL), D_FF ** -0.5),
        'norm_mix': gain((DEPTH, D_MODEL)),
        'w_in': nrm((DEPTH, D_MODEL, D_IN), D_MODEL ** -0.5),
        'hgrn_lb': nrm((DEPTH + 1, FORGET_DIM), 0.5),
        'hgrn_gnorm': gain((DEPTH, A_VDIM)),
        'conv_w': nrm((DEPTH, CONV_W, C_CONV), CONV_W ** -0.5),
        'conv_b': nrm((DEPTH, C_CONV), 0.02),
        'conv_ln_g': gain((DEPTH, C_CONV)),
        'conv_ln_b': nrm((DEPTH, C_CONV), 0.02),
        'w_out': nrm((DEPTH, D_MIX, D_MODEL), D_MIX ** -0.5),
        'norm_xattn': gain((DEPTH, D_MODEL)),
        'norm_mem': gain((DEPTH, D_MODEL)),
        'xattn_wq': nrm((DEPTH, D_MODEL, XA_DIM), D_MODEL ** -0.5),
        'xattn_wk': nrm((DEPTH, D_MODEL, XA_DIM), D_MODEL ** -0.5),
        'xattn_wv': nrm((DEPTH, D_MODEL, XA_DIM), D_MODEL ** -0.5),
        'xattn_wo': nrm((DEPTH, XA_DIM, D_MODEL), XA_DIM ** -0.5),
        'norm_ffn2': gain((DEPTH, D_MODEL)),
        'ffn2_w_gate': nrm((DEPTH, D_MODEL, D_FF), D_MODEL ** -0.5),
        'ffn2_w_up': nrm((DEPTH, D_MODEL, D_FF), D_MODEL ** -0.5),
        'ffn2_w_down': nrm((DEPTH, D_FF, D_MODEL), D_FF ** -0.5),
        'norm_final': gain((D_MODEL,)),
    }


def reference(x_prompt, x_sample, mem_prompt, state_hgrn, state_conv, cache_mem_k, cache_mem_v,
              norm_ffn1, ffn1_w_gate, ffn1_w_up, ffn1_w_down,
              norm_mix, w_in, hgrn_lb, hgrn_gnorm, conv_w, conv_b, conv_ln_g, conv_ln_b, w_out,
              norm_xattn, norm_mem, xattn_wq, xattn_wk, xattn_wv, xattn_wo,
              norm_ffn2, ffn2_w_gate, ffn2_w_up, ffn2_w_down, norm_final):
    lb_all = jnp.cumsum(jax.nn.softmax(hgrn_lb.astype(jnp.float32), axis=0), axis=0)
    B = x_prompt.shape[0]
    xp, xs = x_prompt, x_sample
    sh_p, sc_p, mk_p, mv_p, sh_s, sc_s = [], [], [], [], [], []
    for l in range(DEPTH):
        p = (norm_ffn1[l], ffn1_w_gate[l], ffn1_w_up[l], ffn1_w_down[l],
             norm_mix[l], w_in[l], hgrn_gnorm[l], conv_w[l], conv_b[l], conv_ln_g[l], conv_ln_b[l], w_out[l],
             norm_xattn[l], xattn_wq[l], xattn_wo[l],
             norm_ffn2[l], ffn2_w_gate[l], ffn2_w_up[l], ffn2_w_down[l])
        lb = lb_all[l]
        mk, mv = mem_kv(mem_prompt, norm_mem[l], xattn_wk[l], xattn_wv[l])
        s0 = jnp.zeros((B, A_HEADS, A_EXPAND, A_VDIM), state_hgrn.dtype)
        b0 = jnp.zeros((B, CONV_W - 1, C_CONV), state_conv.dtype)
        xp, s_new, b_new = block(xp, mk, mv, s0, b0, lb, p)
        sh_p.append(s_new); sc_p.append(b_new); mk_p.append(mk.astype(cache_mem_k.dtype)); mv_p.append(mv.astype(cache_mem_v.dtype))
        xs, s_new, b_new = block(xs, cache_mem_k[l], cache_mem_v[l], state_hgrn[l], state_conv[l], lb, p)
        sh_s.append(s_new); sc_s.append(b_new)
    y_prompt = rmsnorm(xp, norm_final)
    y_sample = rmsnorm(xs, norm_final)
    return (y_prompt, y_sample, jnp.stack(sh_p), jnp.stack(sc_p), jnp.stack(mk_p), jnp.stack(mv_p),
            jnp.stack(sh_s), jnp.stack(sc_s))
```

```python
import functools

import numpy as np
import jax
import jax.numpy as jnp
from jax import lax
from jax.experimental import pallas as pl
from jax.experimental.pallas import tpu as pltpu

F32 = jnp.float32
BF16 = jnp.bfloat16
EPS = 1e-6
FFN_RES = 0.5
LANE = 128
ROWS = 128
CONV_W = 31
HIST = CONV_W - 1
VMEM_LIMIT = 56 * 1024 * 1024


def _params(*sem):
    return pltpu.CompilerParams(dimension_semantics=sem, vmem_limit_bytes=VMEM_LIMIT)


def _sigmoid(x):
    return 1.0 / (1.0 + jnp.exp(-x))


def _rms(x, g):
    return x * lax.rsqrt(jnp.mean(x * x, axis=-1, keepdims=True) + EPS) * g


def _dot(a, b):
    return jnp.dot(a, b, preferred_element_type=F32)


def _dot_nt(a, b):
    return lax.dot_general(a, b, (((1,), (1,)), ((), ())), preferred_element_type=F32)


def _dot_tn(a, b):
    return lax.dot_general(a, b, (((0,), (0,)), ((), ())), preferred_element_type=F32)


def _split3(x):
    hi = x.astype(BF16)
    r1 = x - hi.astype(F32)
    mid = r1.astype(BF16)
    lo = (r1 - mid.astype(F32)).astype(BF16)
    return hi, mid, lo


def _ffn_kernel(x_ref, g_ref, wg_ref, wu_ref, wd_ref, gf_ref, o_ref, xn_ref, *, final_norm):
    j = pl.program_id(1)

    @pl.when(j == 0)
    def _():
        xn_ref[...] = _rms(x_ref[...], g_ref[...]).astype(BF16)
        o_ref[...] = jnp.zeros_like(o_ref)

    xn = xn_ref[...]
    g = _dot(xn, wg_ref[...])
    u = _dot(xn, wu_ref[...])
    h = (g * _sigmoid(g) * u).astype(BF16)
    o_ref[...] += _dot(h, wd_ref[...])

    @pl.when(j == pl.num_programs(1) - 1)
    def _():
        y = x_ref[...] + FFN_RES * o_ref[...]
        if final_norm:
            y = _rms(y, gf_ref[...])
        o_ref[...] = y


def _ffn(x, g, wg, wu, wd, gf, *, final_norm, tm=512, tf=512):
    m, d = x.shape
    f = wg.shape[1]
    assert m % tm == 0 and f % tf == 0
    return pl.pallas_call(
        functools.partial(_ffn_kernel, final_norm=final_norm),
        grid=(m // tm, f // tf),
        in_specs=[
            pl.BlockSpec((tm, d), lambda i, j: (i, 0)),
            pl.BlockSpec((1, d), lambda i, j: (0, 0)),
            pl.BlockSpec((d, tf), lambda i, j: (0, j)),
            pl.BlockSpec((d, tf), lambda i, j: (0, j)),
            pl.BlockSpec((tf, d), lambda i, j: (j, 0)),
            pl.BlockSpec((1, d), lambda i, j: (0, 0)),
        ],
        out_specs=pl.BlockSpec((tm, d), lambda i, j: (i, 0)),
        out_shape=jax.ShapeDtypeStruct((m, d), F32),
        scratch_shapes=[pltpu.VMEM((tm, d), BF16)],
        compiler_params=_params("parallel", "arbitrary"),
        name="ffn",
    )(x, g, wg, wu, wd, gf)


def _norm_matmul_kernel(x_ref, g_ref, w_ref, o_ref, xn_ref):
    @pl.when(pl.program_id(1) == 0)
    def _():
        xn_ref[...] = _rms(x_ref[...], g_ref[...]).astype(BF16)

    o_ref[...] = _dot(xn_ref[...], w_ref[...])


def _norm_matmul(x, g, w, *, tm, tn):
    m, d = x.shape
    n = w.shape[1]
    assert m % tm == 0 and n % tn == 0
    return pl.pallas_call(
        _norm_matmul_kernel,
        grid=(m // tm, n // tn),
        in_specs=[
            pl.BlockSpec((tm, d), lambda i, j: (i, 0)),
            pl.BlockSpec((1, d), lambda i, j: (0, 0)),
            pl.BlockSpec((d, tn), lambda i, j: (0, j)),
        ],
        out_specs=pl.BlockSpec((tm, tn), lambda i, j: (i, j)),
        out_shape=jax.ShapeDtypeStruct((m, n), F32),
        scratch_shapes=[pltpu.VMEM((tm, d), BF16)],
        compiler_params=_params("parallel", "arbitrary"),
        name="norm_matmul",
    )(x, g, w)


def _outproj_kernel(x_ref, a_ref, b_ref, wa_ref, wb_ref, o_ref):
    o_ref[...] = x_ref[...] + _dot(a_ref[...], wa_ref[...]) + _dot(b_ref[...], wb_ref[...])


def _outproj(x, a, b, w, *, tm, tn):
    m, d = x.shape
    ka = a.shape[1]
    assert w.shape[0] == 2 * ka and b.shape[1] == ka and m % tm == 0 and d % tn == 0
    return pl.pallas_call(
        _outproj_kernel,
        grid=(m // tm, d // tn),
        in_specs=[
            pl.BlockSpec((tm, tn), lambda i, j: (i, j)),
            pl.BlockSpec((tm, ka), lambda i, j: (i, 0)),
            pl.BlockSpec((tm, ka), lambda i, j: (i, 0)),
            pl.BlockSpec((ka, tn), lambda i, j: (0, j)),
            pl.BlockSpec((ka, tn), lambda i, j: (1, j)),
        ],
        out_specs=pl.BlockSpec((tm, tn), lambda i, j: (i, j)),
        out_shape=jax.ShapeDtypeStruct((m, d), F32),
        compiler_params=_params("parallel", "arbitrary"),
        name="outproj",
    )(x, a, b, w, w)


def _hgrn_structure(bk):
    r = np.arange(ROWS)
    blk = r // bk
    same = blk[:, None] == blk[None, :]
    u = r[None, :]
    t = r[:, None]
    mats = [same & (u <= t), same & (u > t)]
    masks = [np.eye(ROWS, dtype=bool)]
    m = 1
    while m < bk:
        grp = r // (2 * m)
        is_r = (r % (2 * m)) >= m
        mid = (grp * 2 * m + m - 1)[:, None]
        mats.append(np.where(is_r[:, None], (u > mid) & (u <= t), (u > t) & (u <= mid)))
        masks.append((grp[:, None] == grp[None, :]) & is_r[:, None] & (~is_r)[None, :])
        m *= 2
    nblk = ROWS // bk
    rsel = (blk[:, None] == (np.arange(nblk * LANE) // LANE)[None, :])
    return (jnp.asarray(np.concatenate(mats, 0), BF16), jnp.asarray(np.stack(masks), F32),
            jnp.asarray(rsel, F32), len(mats) - 2, nblk)


def _hgrn_kernel(zq_ref, zf_ref, zi_ref, zg_ref, lb_ref, gn_ref, gmat_ref, mask_ref, rsel_ref, s0_ref,
                 o_ref, sout_ref, s_scr, *, nlev, nblk):
    @pl.when(pl.program_id(2) == 0)
    def _():
        s_scr[...] = s0_ref[...]

    zq = zq_ref[...]
    q = zq * _sigmoid(zq)
    lbr = lb_ref[...]
    e = jnp.exp(lbr - jnp.max(lbr, axis=0, keepdims=True))
    lb = e[0:1] / jnp.sum(e, axis=0, keepdims=True)
    f = lb + (1.0 - lb) * _sigmoid(zf_ref[...])
    logf = jnp.log(f)
    k = 1.0 - f
    v16 = zi_ref[...].astype(BF16)

    parts = _split3(logf)
    gmat = gmat_ref[...]
    ee = _dot(gmat, parts[0]) + _dot(gmat, parts[1]) + _dot(gmat, parts[2])
    b = ee[0:ROWS]
    c = ee[ROWS:2 * ROWS]

    a = mask_ref[0] * _dot_nt(q.astype(BF16), k.astype(BF16))
    for l in range(nlev):
        w = jnp.exp(ee[(2 + l) * ROWS:(3 + l) * ROWS])
        a = a + mask_ref[1 + l] * _dot_nt((q * w).astype(BF16), (k * w).astype(BF16))
    o = _dot(a.astype(BF16), v16)

    rsel = rsel_ref[...]
    scat = jnp.concatenate([s_scr[i] for i in range(nblk)], axis=1)
    y = _dot((q * jnp.exp(b)).astype(BF16), scat.astype(BF16)) * rsel
    for i in range(nblk):
        o = o + y[:, i * LANE:(i + 1) * LANE]

    kc = (k * jnp.exp(c)).astype(BF16)
    vblk = (jnp.concatenate([zi_ref[...]] * nblk, axis=1) * rsel).astype(BF16)
    upd = _dot_tn(kc, vblk)
    rs16 = rsel.astype(BF16)
    bend_t = _dot_tn(parts[0], rs16) + _dot_tn(parts[1], rs16) + _dot_tn(parts[2], rs16)
    snew = jnp.exp(bend_t) * scat + upd
    for i in range(nblk):
        s_scr[i] = snew[:, i * LANE:(i + 1) * LANE]
    sout_ref[...] = s_scr[...]

    zg = zg_ref[...]
    o_ref[...] = (_rms(o, gn_ref[...]) * (zg * _sigmoid(zg))).astype(o_ref.dtype)


def _hgrn(z, lbp, gnorm, s0, *, bk, tiles):
    m = z.shape[0]
    nseq, heads = s0.shape[0], s0.shape[1]
    gmat, masks, rsel, nlev, nblk = _hgrn_structure(bk)
    groups = m // (ROWS * tiles)
    assert groups * nblk == nseq and groups * tiles * ROWS == m

    def zspec(part):
        return pl.BlockSpec((ROWS, LANE), lambda g, h, t: (g * tiles + t, part * heads + h))

    const2 = lambda g, h, t: (0, 0)
    sspec = pl.BlockSpec((nblk, None, LANE, LANE), lambda g, h, t: (g, h, 0, 0))
    return pl.pallas_call(
        functools.partial(_hgrn_kernel, nlev=nlev, nblk=nblk),
        grid=(groups, heads, tiles),
        in_specs=[
            zspec(0), zspec(1), zspec(2), zspec(3),
            pl.BlockSpec((lbp.shape[0], LANE), lambda g, h, t: (0, h)),
            pl.BlockSpec((1, LANE), const2),
            pl.BlockSpec(gmat.shape, const2),
            pl.BlockSpec(masks.shape, lambda g, h, t: (0, 0, 0)),
            pl.BlockSpec(rsel.shape, const2),
            sspec,
        ],
        out_specs=[pl.BlockSpec((ROWS, LANE), lambda g, h, t: (g * tiles + t, h)), sspec],
        out_shape=[jax.ShapeDtypeStruct((m, heads * LANE), BF16),
                   jax.ShapeDtypeStruct(s0.shape, s0.dtype)],
        scratch_shapes=[pltpu.VMEM((nblk, LANE, LANE), F32)],
        compiler_params=_params("parallel", "parallel", "arbitrary"),
        name="hgrn",
    )(z, z, z, z, lbp, gnorm, gmat, masks, rsel, s0)


def _conv_kernel(za_ref, zb_ref, buf_ref, w_ref, cb_ref, lg_ref, lb_ref, o_ref, bufout_ref, ucat, dw,
                 *, ns, ls, nc):
    u = za_ref[...] * _sigmoid(zb_ref[...])

    @pl.when(pl.program_id(1) == 0)
    def _():
        for i in range(ns):
            for cb in range(nc):
                ucat[i, cb, 0:HIST, :] = buf_ref[i, :, cb * LANE:(cb + 1) * LANE]

    for i in range(ns):
        for cb in range(nc):
            ucat[i, cb, HIST:HIST + ls, :] = u[i * ls:(i + 1) * ls, cb * LANE:(cb + 1) * LANE]

    rt = min(ls, 64)

    def cb_body(cb, carry):
        for i in range(ns):
            for r0 in range(0, ls, rt):
                acc = jnp.broadcast_to(cb_ref[cb], (rt, LANE))
                for j in range(CONV_W):
                    acc = acc + ucat[i, cb, r0 + j:r0 + j + rt, :] * w_ref[cb, j:j + 1, :]
                dw[i, cb, r0:r0 + rt, :] = acc
        return carry

    lax.fori_loop(0, nc, cb_body, 0)

    rt2 = min(ls, 32)
    inv_c = 1.0 / (nc * LANE)
    for i in range(ns):
        for r0 in range(0, ls, rt2):
            x = dw[i, :, r0:r0 + rt2, :]
            mu = jnp.sum(jnp.sum(x, axis=0), axis=-1, keepdims=True) * inv_c
            xc = x - mu[None]
            var = jnp.sum(jnp.sum(xc * xc, axis=0), axis=-1, keepdims=True) * inv_c
            y = xc * lax.rsqrt(var + EPS)[None] * lg_ref[...] + lb_ref[...]
            y = y * _sigmoid(y)
            for cb in range(nc):
                o_ref[i * ls + r0:i * ls + r0 + rt2, cb * LANE:(cb + 1) * LANE] = y[cb].astype(o_ref.dtype)

    for i in range(ns):
        for cb in range(nc):
            tail = ucat[i, cb, ls:ls + HIST, :]
            bufout_ref[i, :, cb * LANE:(cb + 1) * LANE] = tail
            ucat[i, cb, 0:HIST, :] = tail


def _conv(z, buf0, w3, cb3, lg3, lb3, *, ns, ls, tiles, za_col):
    m = z.shape[0]
    nseq, _, c = buf0.shape
    nc = c // LANE
    groups = nseq // ns
    assert groups * tiles * ns * ls == m
    rows = ns * ls
    pad_rows = -(-(HIST + ls) // 8) * 8
    bspec = pl.BlockSpec((ns, HIST, c), lambda g, t: (g, 0, 0))
    const3 = lambda g, t: (0, 0, 0)
    return pl.pallas_call(
        functools.partial(_conv_kernel, ns=ns, ls=ls, nc=nc),
        grid=(groups, tiles),
        in_specs=[
            pl.BlockSpec((rows, c), lambda g, t: (g * tiles + t, za_col)),
            pl.BlockSpec((rows, c), lambda g, t: (g * tiles + t, za_col + 1)),
            bspec,
            pl.BlockSpec(w3.shape, const3),
            pl.BlockSpec(cb3.shape, const3),
            pl.BlockSpec(lg3.shape, const3),
            pl.BlockSpec(lb3.shape, const3),
        ],
        out_specs=[pl.BlockSpec((rows, c), lambda g, t: (g * tiles + t, 0)), bspec],
        out_shape=[jax.ShapeDtypeStruct((m, c), BF16), jax.ShapeDtypeStruct(buf0.shape, buf0.dtype)],
        scratch_shapes=[pltpu.VMEM((ns, nc, pad_rows, LANE), F32), pltpu.VMEM((ns, nc, ls, LANE), F32)],
        compiler_params=_params("parallel", "arbitrary"),
        name="conv",
    )(z, z, buf0, w3, cb3, lg3, lb3)


def _xattn_kernel(x_ref, g_ref, wq_ref, k_ref, v_ref, wo_ref, o_ref, *, nb, lq, heads):
    x = x_ref[...]
    q = _dot(_rms(x, g_ref[...]).astype(BF16), wq_ref[...])
    mem = k_ref.shape[0] // nb
    k = k_ref[...].astype(BF16)
    v = v_ref[...].astype(BF16)
    hd = q.shape[1] // heads
    scale = hd ** -0.5
    if nb > 1:
        rows = lax.broadcasted_iota(jnp.int32, (nb * lq, nb * mem), 0) // lq
        cols = lax.broadcasted_iota(jnp.int32, (nb * lq, nb * mem), 1) // mem
        own = rows == cols
    outs = []
    for h in range(heads):
        sl = slice(h * hd, (h + 1) * hd)
        s = _dot_nt(q[:, sl].astype(BF16), k[:, sl]) * scale
        if nb > 1:
            s = jnp.where(own, s, -1e30)
        p = jnp.exp(s - jnp.max(s, axis=-1, keepdims=True))
        l = jnp.sum(p, axis=-1, keepdims=True)
        outs.append((_dot(p.astype(BF16), v[:, sl]) / l).astype(BF16))
    o_ref[...] = x + _dot(jnp.concatenate(outs, axis=1), wo_ref[...])


def _xattn(x, g, wq, k2d, v2d, wo, *, nb, lq, mem, heads):
    m, d = x.shape
    xa = wq.shape[1]
    rows = nb * lq
    assert m % rows == 0
    steps = m // rows
    nseq = k2d.shape[0] // mem
    tiles = steps * nb // nseq
    assert tiles * nseq == steps * nb and (nb == 1 or tiles == 1)
    kv_map = lambda i: (i // tiles, 0)
    const = lambda i: (0, 0)
    return pl.pallas_call(
        functools.partial(_xattn_kernel, nb=nb, lq=lq, heads=heads),
        grid=(steps,),
        in_specs=[
            pl.BlockSpec((rows, d), lambda i: (i, 0)),
            pl.BlockSpec((1, d), const),
            pl.BlockSpec((d, xa), const),
            pl.BlockSpec((nb * mem, xa), kv_map),
            pl.BlockSpec((nb * mem, xa), kv_map),
            pl.BlockSpec((xa, d), const),
        ],
        out_specs=pl.BlockSpec((rows, d), lambda i: (i, 0)),
        out_shape=jax.ShapeDtypeStruct((m, d), F32),
        compiler_params=_params("parallel"),
        name="xattn",
    )(x, g, wq, k2d, v2d, wo)


def kernel(x_prompt, x_sample, mem_prompt, state_hgrn, state_conv, cache_mem_k, cache_mem_v, norm_ffn1, ffn1_w_gate, ffn1_w_up, ffn1_w_down, norm_mix, w_in, hgrn_lb, hgrn_gnorm, conv_w, conv_b, conv_ln_g, conv_ln_b, w_out, norm_xattn, norm_mem, xattn_wq, xattn_wk, xattn_wv, xattn_wo, norm_ffn2, ffn2_w_gate, ffn2_w_up, ffn2_w_down, norm_final):
    depth = norm_ffn1.shape[0]
    assert depth == 1
    nb_p, seq, d = x_prompt.shape
    nb_s, dseq, _ = x_sample.shape
    mem = mem_prompt.shape[1]
    heads_a, expand, vdim = state_hgrn.shape[2:]
    c_conv = state_conv.shape[-1]
    xa_heads, xa_hd = cache_mem_k.shape[3:]
    xa = xa_heads * xa_hd
    assert expand == LANE and vdim == LANE and c_conv == heads_a * LANE
    za_col = (2 * heads_a * expand + 2 * heads_a * vdim) // c_conv

    l = 0
    row = lambda p: p.reshape(1, -1).astype(F32)
    bf = lambda w: w[l].astype(BF16)
    f1 = (row(norm_ffn1[l]), bf(ffn1_w_gate), bf(ffn1_w_up), bf(ffn1_w_down))
    f2 = (row(norm_ffn2[l]), bf(ffn2_w_gate), bf(ffn2_w_up), bf(ffn2_w_down))
    w_in16, w_out16 = bf(w_in), bf(w_out)
    wq16, wk16, wv16, wo16 = bf(xattn_wq), bf(xattn_wk), bf(xattn_wv), bf(xattn_wo)
    n_mix, n_xa, n_mem, n_fin = row(norm_mix[l]), row(norm_xattn[l]), row(norm_mem[l]), row(norm_final)
    lbp = hgrn_lb.astype(F32)
    gnorm = row(hgrn_gnorm[l])
    nc = c_conv // LANE
    chan3 = lambda p: p.reshape(-1, nc, LANE).transpose(1, 0, 2).astype(F32)
    w3, cb3, lg3, lb3 = chan3(conv_w[l]), chan3(conv_b[l]), chan3(conv_ln_g[l]), chan3(conv_ln_b[l])
    ones_d = jnp.ones((1, d), F32)

    def block(x, k2d, v2d, s0, buf0, *, hg, cv, xat, tm):
        x = _ffn(x, *f1, ones_d, final_norm=False)
        z = _norm_matmul(x, n_mix, w_in16, tm=tm, tn=1024)
        o_a, s_new = _hgrn(z, lbp, gnorm, s0, **hg)
        o_b, buf_new = _conv(z, buf0, w3, cb3, lg3, lb3, za_col=za_col, **cv)
        x = _outproj(x, o_a, o_b, w_out16, tm=tm, tn=1024)
        x = _xattn(x, n_xa, wq16, k2d, v2d, wo16, mem=mem, heads=xa_heads, **xat)
        y = _ffn(x, *f2, n_fin, final_norm=True)
        return y, s_new, buf_new

    memx = mem_prompt.reshape(nb_p * mem, d)
    mk = _norm_matmul(memx, n_mem, wk16, tm=512, tn=xa)
    mv = _norm_matmul(memx, n_mem, wv16, tm=512, tn=xa)
    s0p = jnp.zeros((nb_p, heads_a, expand, vdim), state_hgrn.dtype)
    b0p = jnp.zeros((nb_p, HIST, c_conv), state_conv.dtype)
    yp, shp, scp = block(
        x_prompt.reshape(nb_p * seq, d), mk, mv, s0p, b0p,
        hg=dict(bk=ROWS, tiles=seq // ROWS),
        cv=dict(ns=1, ls=256, tiles=seq // 256),
        xat=dict(nb=1, lq=512), tm=1024)

    ys, shs, scs = block(
        x_sample.reshape(nb_s * dseq, d),
        cache_mem_k[l].reshape(nb_s * mem, xa), cache_mem_v[l].reshape(nb_s * mem, xa),
        state_hgrn[l], state_conv[l],
        hg=dict(bk=dseq, tiles=1),
        cv=dict(ns=8, ls=dseq, tiles=1),
        xat=dict(nb=8, lq=dseq), tm=1024)

    return (yp.reshape(nb_p, seq, d), ys.reshape(nb_s, dseq, d),
            shp[None], scp[None],
            mk.reshape(1, nb_p, mem, xa_heads, xa_hd).astype(cache_mem_k.dtype),
            mv.reshape(1, nb_p, mem, xa_heads, xa_hd).astype(cache_mem_v.dtype),
            shs[None], scs[None])
```

```python
import functools

import numpy as np
import jax
import jax.numpy as jnp
from jax import lax
from jax.experimental import pallas as pl
from jax.experimental.pallas import tpu as pltpu

F32 = jnp.float32
BF16 = jnp.bfloat16
EPS = 1e-6
FFN_RES = 0.5
LANE = 128
ROWS = 128
CONV_W = 31
HIST = CONV_W - 1
VMEM_LIMIT = 56 * 1024 * 1024


def _params(*sem):
    return pltpu.CompilerParams(dimension_semantics=sem, vmem_limit_bytes=VMEM_LIMIT)


def _sigmoid(x):
    return 1.0 / (1.0 + jnp.exp(-x))


def _rms(x, g):
    return x * lax.rsqrt(jnp.mean(x * x, axis=-1, keepdims=True) + EPS) * g


def _dot(a, b):
    return jnp.dot(a, b, preferred_element_type=F32)


def _dot_nt(a, b):
    return lax.dot_general(a, b, (((1,), (1,)), ((), ())), preferred_element_type=F32)


def _dot_tn(a, b):
    return lax.dot_general(a, b, (((0,), (0,)), ((), ())), preferred_element_type=F32)


def _split3(x):
    hi = x.astype(BF16)
    r1 = x - hi.astype(F32)
    mid = r1.astype(BF16)
    lo = (r1 - mid.astype(F32)).astype(BF16)
    return hi, mid, lo


def _ffn_kernel(x_ref, g_ref, wg_ref, wu_ref, wd_ref, gf_ref, o_ref, xn_ref, *, final_norm):
    j = pl.program_id(1)

    @pl.when(j == 0)
    def _():
        xn_ref[...] = _rms(x_ref[...], g_ref[...]).astype(BF16)
        o_ref[...] = jnp.zeros_like(o_ref)

    xn = xn_ref[...]
    g = _dot(xn, wg_ref[...])
    u = _dot(xn, wu_ref[...])
    h = (g * _sigmoid(g) * u).astype(BF16)
    o_ref[...] += _dot(h, wd_ref[...])

    @pl.when(j == pl.num_programs(1) - 1)
    def _():
        y = x_ref[...] + FFN_RES * o_ref[...]
        if final_norm:
            y = _rms(y, gf_ref[...])
        o_ref[...] = y


def _ffn(x, g, wg, wu, wd, gf, *, final_norm, tm=1024, tf=512):
    m, d = x.shape
    f = wg.shape[1]
    assert m % tm == 0 and f % tf == 0
    once = pl.Buffered(1)
    return pl.pallas_call(
        functools.partial(_ffn_kernel, final_norm=final_norm),
        grid=(m // tm, f // tf),
        in_specs=[
            pl.BlockSpec((tm, d), lambda i, j: (i, 0), pipeline_mode=once),
            pl.BlockSpec((1, d), lambda i, j: (0, 0)),
            pl.BlockSpec((d, tf), lambda i, j: (0, j)),
            pl.BlockSpec((d, tf), lambda i, j: (0, j)),
            pl.BlockSpec((tf, d), lambda i, j: (j, 0)),
            pl.BlockSpec((1, d), lambda i, j: (0, 0)),
        ],
        out_specs=pl.BlockSpec((tm, d), lambda i, j: (i, 0), pipeline_mode=once),
        out_shape=jax.ShapeDtypeStruct((m, d), F32),
        scratch_shapes=[pltpu.VMEM((tm, d), BF16)],
        compiler_params=_params("parallel", "arbitrary"),
        name="ffn",
    )(x, g, wg, wu, wd, gf)


def _norm_matmul_kernel(x_ref, g_ref, w_ref, o_ref, xn_ref):
    @pl.when(pl.program_id(1) == 0)
    def _():
        xn_ref[...] = _rms(x_ref[...], g_ref[...]).astype(BF16)

    o_ref[...] = _dot(xn_ref[...], w_ref[...])


def _norm_matmul(x, g, w, *, tm, tn):
    m, d = x.shape
    n = w.shape[1]
    assert m % tm == 0 and n % tn == 0
    return pl.pallas_call(
        _norm_matmul_kernel,
        grid=(m // tm, n // tn),
        in_specs=[
            pl.BlockSpec((tm, d), lambda i, j: (i, 0)),
            pl.BlockSpec((1, d), lambda i, j: (0, 0)),
            pl.BlockSpec((d, tn), lambda i, j: (0, j)),
        ],
        out_specs=pl.BlockSpec((tm, tn), lambda i, j: (i, j)),
        out_shape=jax.ShapeDtypeStruct((m, n), F32),
        scratch_shapes=[pltpu.VMEM((tm, d), BF16)],
        compiler_params=_params("parallel", "arbitrary"),
        name="norm_matmul",
    )(x, g, w)


def _outproj_kernel(x_ref, a_ref, b_ref, wa_ref, wb_ref, o_ref):
    o_ref[...] = x_ref[...] + _dot(a_ref[...], wa_ref[...]) + _dot(b_ref[...], wb_ref[...])


def _outproj(x, a, b, w, *, tm, tn):
    m, d = x.shape
    ka = a.shape[1]
    assert w.shape[0] == 2 * ka and b.shape[1] == ka and m % tm == 0 and d % tn == 0
    return pl.pallas_call(
        _outproj_kernel,
        grid=(m // tm, d // tn),
        in_specs=[
            pl.BlockSpec((tm, tn), lambda i, j: (i, j)),
            pl.BlockSpec((tm, ka), lambda i, j: (i, 0)),
            pl.BlockSpec((tm, ka), lambda i, j: (i, 0)),
            pl.BlockSpec((ka, tn), lambda i, j: (0, j)),
            pl.BlockSpec((ka, tn), lambda i, j: (1, j)),
        ],
        out_specs=pl.BlockSpec((tm, tn), lambda i, j: (i, j)),
        out_shape=jax.ShapeDtypeStruct((m, d), F32),
        compiler_params=_params("parallel", "arbitrary"),
        name="outproj",
    )(x, a, b, w, w)


def _hgrn_structure(bk):
    r = np.arange(ROWS)
    blk = r // bk
    tri = (blk[:, None] == blk[None, :]) & (r[None, :] <= r[:, None])
    masks = [np.eye(ROWS, dtype=bool)]
    m = 1
    while m < bk:
        grp = r // (2 * m)
        is_r = (r % (2 * m)) >= m
        masks.append((grp[:, None] == grp[None, :]) & is_r[:, None] & (~is_r)[None, :])
        m *= 2
    nblk = ROWS // bk
    rsel = blk[:, None] == (np.arange(nblk * LANE) // LANE)[None, :]
    return jnp.asarray(tri, BF16), jnp.asarray(np.stack(masks), F32), jnp.asarray(rsel, F32), nblk


def _hgrn_gates(zf_ref, lb_ref, tri_ref):
    lbr = lb_ref[...]
    e = jnp.exp(lbr - jnp.max(lbr, axis=0, keepdims=True))
    lb = e[0:1] / jnp.sum(e, axis=0, keepdims=True)
    f = lb + (1.0 - lb) * _sigmoid(zf_ref[...])
    parts = _split3(jnp.log(f))
    tri = tri_ref[...]
    b = _dot(tri, parts[0]) + _dot(tri, parts[1]) + _dot(tri, parts[2])
    return f, parts, b


def _level_weights(f, b, row, bk):
    ws = []
    m = 1
    while m < bk:
        is_r = (row & (2 * m - 1)) >= m
        if m == 1:
            ws.append(jnp.where(is_r, f, 1.0))
        else:
            g = max(2 * m, 8)
            bg = b.reshape(ROWS // g, g, LANE)
            if 2 * m >= 8:
                ref = bg[:, m - 1:m, :]
            else:
                sub = row.reshape(ROWS // 8, 8, LANE) & 7
                ref = jnp.where(sub < 4, bg[:, 1:2, :], bg[:, 5:6, :])
            d = (bg - ref).reshape(ROWS, LANE)
            ws.append(jnp.exp(jnp.where(is_r, d, -d)))
        m *= 2
    return ws


def _intra_scores(q, k, ws, mask_ref):
    a = mask_ref[0] * _dot_nt(q.astype(BF16), k.astype(BF16))
    for l, w in enumerate(ws):
        a = a + mask_ref[1 + l] * _dot_nt((q * w).astype(BF16), (k * w).astype(BF16))
    return a.astype(BF16)


def _hgrn_seq_kernel(zq_ref, zf_ref, zi_ref, zg_ref, lb_ref, gn_ref, tri_ref, mask_ref, s0_ref,
                     o_ref, sout_ref, st_scr, *, heads):
    t = pl.program_id(1)

    @pl.when(t == 0)
    def _():
        for h in range(heads):
            st_scr[h] = s0_ref[0, h].T

    f_all, _, b_all = _hgrn_gates(zf_ref, lb_ref, tri_ref)
    row = lax.broadcasted_iota(jnp.int32, (ROWS, LANE), 0)
    for h in range(heads):
        sl = slice(h * LANE, (h + 1) * LANE)
        zq = zq_ref[:, sl]
        q = zq * _sigmoid(zq)
        f = f_all[:, sl]
        k = 1.0 - f
        b = b_all[:, sl]
        a = _intra_scores(q, k, _level_weights(f, b, row, ROWS), mask_ref)
        v16 = zi_ref[:, sl].astype(BF16)
        st = st_scr[h]
        bend = b[ROWS - 1:ROWS, :]
        o = _dot(a, v16) + _dot_nt((q * jnp.exp(b)).astype(BF16), st.astype(BF16))
        kc = (k * jnp.exp(bend - b)).astype(BF16)
        st_scr[h] = jnp.exp(bend) * st + _dot_tn(v16, kc)
        zg = zg_ref[:, sl]
        o_ref[:, sl] = (_rms(o, gn_ref[...]) * (zg * _sigmoid(zg))).astype(o_ref.dtype)

    @pl.when(t == pl.num_programs(1) - 1)
    def _():
        for h in range(heads):
            sout_ref[0, h] = st_scr[h].T


def _hgrn_blocks_kernel(zq_ref, zf_ref, zi_ref, zg_ref, lb_ref, gn_ref, tri_ref, mask_ref, rsel_ref, s0_ref,
                        o_ref, sout_ref, *, heads, bk, nblk):
    f_all, parts, b_all = _hgrn_gates(zf_ref, lb_ref, tri_ref)
    row = lax.broadcasted_iota(jnp.int32, (ROWS, LANE), 0)
    rsel = rsel_ref[...]
    rs16 = rsel.astype(BF16)
    for h in range(heads):
        sl = slice(h * LANE, (h + 1) * LANE)
        zq = zq_ref[:, sl]
        q = zq * _sigmoid(zq)
        f = f_all[:, sl]
        k = 1.0 - f
        b = b_all[:, sl]
        a = _intra_scores(q, k, _level_weights(f, b, row, bk), mask_ref)
        v = zi_ref[:, sl]
        scat = jnp.concatenate([s0_ref[i, h] for i in range(nblk)], axis=1)
        y = _dot((q * jnp.exp(b)).astype(BF16), scat.astype(BF16))
        o = _dot(a, v.astype(BF16)) + jnp.concatenate(
            [y[i * bk:(i + 1) * bk, i * LANE:(i + 1) * LANE] for i in range(nblk)], axis=0)
        bg = b.reshape(nblk, bk, LANE)
        c = (bg[:, bk - 1:bk, :] - bg).reshape(ROWS, LANE)
        kc = (k * jnp.exp(c)).astype(BF16)
        vblk = (jnp.concatenate([v] * nblk, axis=1) * rsel).astype(BF16)
        bend_t = (_dot_tn(parts[0][:, sl], rs16) + _dot_tn(parts[1][:, sl], rs16)
                  + _dot_tn(parts[2][:, sl], rs16))
        snew = jnp.exp(bend_t) * scat + _dot_tn(kc, vblk)
        for i in range(nblk):
            sout_ref[i, h] = snew[:, i * LANE:(i + 1) * LANE]
        zg = zg_ref[:, sl]
        o_ref[:, sl] = (_rms(o, gn_ref[...]) * (zg * _sigmoid(zg))).astype(o_ref.dtype)


def _hgrn(z, lbp, gnorm, s0, *, bk, tiles, hs):
    m = z.shape[0]
    nseq, heads = s0.shape[0], s0.shape[1]
    tri, masks, rsel, nblk = _hgrn_structure(bk)
    groups = m // (ROWS * tiles)
    assert groups * nblk == nseq and groups * tiles * ROWS == m and heads % hs == 0
    hb = heads // hs
    w = hs * LANE
    seq_mode = nblk == 1

    def zspec(part):
        if seq_mode:
            return pl.BlockSpec((ROWS, w), lambda g, j: (g * tiles + j, part))
        return pl.BlockSpec((ROWS, w), lambda g, j: (g, part * hb + j))

    const2 = lambda g, j: (0, 0)
    const3 = lambda g, j: (0, 0, 0)
    common = [zspec(0), zspec(1), zspec(2), zspec(3),
              pl.BlockSpec((lbp.shape[0], w), const2 if seq_mode else (lambda g, j: (0, j))),
              pl.BlockSpec((1, LANE), const2),
              pl.BlockSpec(tri.shape, const2),
              pl.BlockSpec(masks.shape, const3)]
    out_shape = [jax.ShapeDtypeStruct((m, heads * LANE), BF16), jax.ShapeDtypeStruct(s0.shape, s0.dtype)]
    if seq_mode:
        assert hs == heads
        sspec = pl.BlockSpec((1, heads, LANE, LANE), lambda g, j: (g, 0, 0, 0))
        return pl.pallas_call(
            functools.partial(_hgrn_seq_kernel, heads=heads),
            grid=(groups, tiles),
            in_specs=common + [sspec],
            out_specs=[pl.BlockSpec((ROWS, w), lambda g, j: (g * tiles + j, 0)), sspec],
            out_shape=out_shape,
            scratch_shapes=[pltpu.VMEM((heads, LANE, LANE), F32)],
            compiler_params=_params("parallel", "arbitrary"),
            name="hgrn_seq",
        )(z, z, z, z, lbp, gnorm, tri, masks, s0)
    assert tiles == 1
    sspec = pl.BlockSpec((nblk, hs, LANE, LANE), lambda g, j: (g, j, 0, 0))
    return pl.pallas_call(
        functools.partial(_hgrn_blocks_kernel, heads=hs, bk=bk, nblk=nblk),
        grid=(groups, hb),
        in_specs=common + [pl.BlockSpec(rsel.shape, const2), sspec],
        out_specs=[pl.BlockSpec((ROWS, w), lambda g, j: (g, j)), sspec],
        out_shape=out_shape,
        compiler_params=_params("parallel", "parallel"),
        name="hgrn_blocks",
    )(z, z, z, z, lbp, gnorm, tri, masks, rsel, s0)


def _conv_kernel(za_ref, zb_ref, buf_ref, w_ref, cb_ref, lg_ref, lb_ref, o_ref, bufout_ref, ucat, dw,
                 *, ns, ls, nc):
    u = za_ref[...] * _sigmoid(zb_ref[...])

    @pl.when(pl.program_id(1) == 0)
    def _():
        for i in range(ns):
            for cb in range(nc):
                ucat[i, cb, 0:HIST, :] = buf_ref[i, :, cb * LANE:(cb + 1) * LANE]

    for i in range(ns):
        for cb in range(nc):
            ucat[i, cb, HIST:HIST + ls, :] = u[i * ls:(i + 1) * ls, cb * LANE:(cb + 1) * LANE]

    rt = min(ls, 64)

    def cb_body(cb, carry):
        for i in range(ns):
            for r0 in range(0, ls, rt):
                acc = jnp.broadcast_to(cb_ref[cb], (rt, LANE))
                for j in range(CONV_W):
                    acc = acc + ucat[i, cb, r0 + j:r0 + j + rt, :] * w_ref[cb, j:j + 1, :]
                dw[i, cb, r0:r0 + rt, :] = acc
        return carry

    lax.fori_loop(0, nc, cb_body, 0)

    rt2 = min(ls, 32)
    inv_c = 1.0 / (nc * LANE)
    for i in range(ns):
        for r0 in range(0, ls, rt2):
            x = dw[i, :, r0:r0 + rt2, :]
            mu = jnp.sum(jnp.sum(x, axis=0), axis=-1, keepdims=True) * inv_c
            xc = x - mu[None]
            var = jnp.sum(jnp.sum(xc * xc, axis=0), axis=-1, keepdims=True) * inv_c
            y = xc * lax.rsqrt(var + EPS)[None] * lg_ref[...] + lb_ref[...]
            y = y * _sigmoid(y)
            for cb in range(nc):
                o_ref[i * ls + r0:i * ls + r0 + rt2, cb * LANE:(cb + 1) * LANE] = y[cb].astype(o_ref.dtype)

    for i in range(ns):
        for cb in range(nc):
            tail = ucat[i, cb, ls:ls + HIST, :]
            bufout_ref[i, :, cb * LANE:(cb + 1) * LANE] = tail
            ucat[i, cb, 0:HIST, :] = tail


def _conv(z, buf0, w3, cb3, lg3, lb3, *, ns, ls, tiles, za_col):
    m = z.shape[0]
    nseq, _, c = buf0.shape
    nc = c // LANE
    groups = nseq // ns
    assert groups * tiles * ns * ls == m
    rows = ns * ls
    pad_rows = -(-(HIST + ls) // 8) * 8
    bspec = pl.BlockSpec((ns, HIST, c), lambda g, t: (g, 0, 0))
    const3 = lambda g, t: (0, 0, 0)
    return pl.pallas_call(
        functools.partial(_conv_kernel, ns=ns, ls=ls, nc=nc),
        grid=(groups, tiles),
        in_specs=[
            pl.BlockSpec((rows, c), lambda g, t: (g * tiles + t, za_col)),
            pl.BlockSpec((rows, c), lambda g, t: (g * tiles + t, za_col + 1)),
            bspec,
            pl.BlockSpec(w3.shape, const3),
            pl.BlockSpec(cb3.shape, const3),
            pl.BlockSpec(lg3.shape, const3),
            pl.BlockSpec(lb3.shape, const3),
        ],
        out_specs=[pl.BlockSpec((rows, c), lambda g, t: (g * tiles + t, 0)), bspec],
        out_shape=[jax.ShapeDtypeStruct((m, c), BF16), jax.ShapeDtypeStruct(buf0.shape, buf0.dtype)],
        scratch_shapes=[pltpu.VMEM((ns, nc, pad_rows, LANE), F32), pltpu.VMEM((ns, nc, ls, LANE), F32)],
        compiler_params=_params("parallel", "arbitrary"),
        name="conv",
    )(z, z, buf0, w3, cb3, lg3, lb3)


def _xattn_kernel(x_ref, g_ref, wq_ref, k_ref, v_ref, wo_ref, o_ref, *, nb, lq, heads):
    x = x_ref[...]
    q = _dot(_rms(x, g_ref[...]).astype(BF16), wq_ref[...])
    mem = k_ref.shape[0] // nb
    k = k_ref[...].astype(BF16)
    v = v_ref[...].astype(BF16)
    hd = q.shape[1] // heads
    scale = hd ** -0.5
    if nb > 1:
        rows = lax.broadcasted_iota(jnp.int32, (nb * lq, nb * mem), 0) // lq
        cols = lax.broadcasted_iota(jnp.int32, (nb * lq, nb * mem), 1) // mem
        own = rows == cols
    outs = []
    for h in range(heads):
        sl = slice(h * hd, (h + 1) * hd)
        s = _dot_nt(q[:, sl].astype(BF16), k[:, sl]) * scale
        if nb > 1:
            s = jnp.where(own, s, -1e30)
        p = jnp.exp(s - jnp.max(s, axis=-1, keepdims=True))
        l = jnp.sum(p, axis=-1, keepdims=True)
        outs.append((_dot(p.astype(BF16), v[:, sl]) / l).astype(BF16))
    o_ref[...] = x + _dot(jnp.concatenate(outs, axis=1), wo_ref[...])


def _xattn(x, g, wq, k2d, v2d, wo, *, nb, lq, mem, heads):
    m, d = x.shape
    xa = wq.shape[1]
    rows = nb * lq
    assert m % rows == 0
    steps = m // rows
    nseq = k2d.shape[0] // mem
    tiles = steps * nb // nseq
    assert tiles * nseq == steps * nb and (nb == 1 or tiles == 1)
    kv_map = lambda i: (i // tiles, 0)
    const = lambda i: (0, 0)
    return pl.pallas_call(
        functools.partial(_xattn_kernel, nb=nb, lq=lq, heads=heads),
        grid=(steps,),
        in_specs=[
            pl.BlockSpec((rows, d), lambda i: (i, 0)),
            pl.BlockSpec((1, d), const),
            pl.BlockSpec((d, xa), const),
            pl.BlockSpec((nb * mem, xa), kv_map),
            pl.BlockSpec((nb * mem, xa), kv_map),
            pl.BlockSpec((xa, d), const),
        ],
        out_specs=pl.BlockSpec((rows, d), lambda i: (i, 0)),
        out_shape=jax.ShapeDtypeStruct((m, d), F32),
        compiler_params=_params("parallel"),
        name="xattn",
    )(x, g, wq, k2d, v2d, wo)


def kernel(x_prompt, x_sample, mem_prompt, state_hgrn, state_conv, cache_mem_k, cache_mem_v, norm_ffn1, ffn1_w_gate, ffn1_w_up, ffn1_w_down, norm_mix, w_in, hgrn_lb, hgrn_gnorm, conv_w, conv_b, conv_ln_g, conv_ln_b, w_out, norm_xattn, norm_mem, xattn_wq, xattn_wk, xattn_wv, xattn_wo, norm_ffn2, ffn2_w_gate, ffn2_w_up, ffn2_w_down, norm_final):
    depth = norm_ffn1.shape[0]
    assert depth == 1
    nb_p, seq, d = x_prompt.shape
    nb_s, dseq, _ = x_sample.shape
    mem = mem_prompt.shape[1]
    heads_a, expand, vdim = state_hgrn.shape[2:]
    c_conv = state_conv.shape[-1]
    xa_heads, xa_hd = cache_mem_k.shape[3:]
    xa = xa_heads * xa_hd
    assert expand == LANE and vdim == LANE and c_conv == heads_a * LANE
    za_col = (2 * heads_a * expand + 2 * heads_a * vdim) // c_conv

    l = 0
    row = lambda p: p.reshape(1, -1).astype(F32)
    bf = lambda w: w[l].astype(BF16)
    f1 = (row(norm_ffn1[l]), bf(ffn1_w_gate), bf(ffn1_w_up), bf(ffn1_w_down))
    f2 = (row(norm_ffn2[l]), bf(ffn2_w_gate), bf(ffn2_w_up), bf(ffn2_w_down))
    w_in16, w_out16 = bf(w_in), bf(w_out)
    wq16, wk16, wv16, wo16 = bf(xattn_wq), bf(xattn_wk), bf(xattn_wv), bf(xattn_wo)
    n_mix, n_xa, n_mem, n_fin = row(norm_mix[l]), row(norm_xattn[l]), row(norm_mem[l]), row(norm_final)
    lbp = hgrn_lb.astype(F32)
    gnorm = row(hgrn_gnorm[l])
    nc = c_conv // LANE
    chan3 = lambda p: p.reshape(-1, nc, LANE).transpose(1, 0, 2).astype(F32)
    w3, cb3, lg3, lb3 = chan3(conv_w[l]), chan3(conv_b[l]), chan3(conv_ln_g[l]), chan3(conv_ln_b[l])
    ones_d = jnp.ones((1, d), F32)

    def block(x, k2d, v2d, s0, buf0, *, hg, cv, xat, tm):
        x = _ffn(x, *f1, ones_d, final_norm=False)
        z = _norm_matmul(x, n_mix, w_in16, tm=tm, tn=1024)
        o_a, s_new = _hgrn(z, lbp, gnorm, s0, **hg)
        o_b, buf_new = _conv(z, buf0, w3, cb3, lg3, lb3, za_col=za_col, **cv)
        x = _outproj(x, o_a, o_b, w_out16, tm=tm, tn=1024)
        x = _xattn(x, n_xa, wq16, k2d, v2d, wo16, mem=mem, heads=xa_heads, **xat)
        y = _ffn(x, *f2, n_fin, final_norm=True)
        return y, s_new, buf_new

    memx = mem_prompt.reshape(nb_p * mem, d)
    mk = _norm_matmul(memx, n_mem, wk16, tm=512, tn=xa)
    mv = _norm_matmul(memx, n_mem, wv16, tm=512, tn=xa)
    s0p = jnp.zeros((nb_p, heads_a, expand, vdim), state_hgrn.dtype)
    b0p = jnp.zeros((nb_p, HIST, c_conv), state_conv.dtype)
    yp, shp, scp = block(
        x_prompt.reshape(nb_p * seq, d), mk, mv, s0p, b0p,
        hg=dict(bk=ROWS, tiles=seq // ROWS, hs=heads_a),
        cv=dict(ns=1, ls=256, tiles=seq // 256),
        xat=dict(nb=1, lq=512), tm=1024)

    ys, shs, scs = block(
        x_sample.reshape(nb_s * dseq, d),
        cache_mem_k[l].astype(BF16).reshape(nb_s * mem, xa),
        cache_mem_v[l].astype(BF16).reshape(nb_s * mem, xa),
        state_hgrn[l], state_conv[l],
        hg=dict(bk=dseq, tiles=1, hs=4),
        cv=dict(ns=8, ls=dseq, tiles=1),
        xat=dict(nb=8, lq=dseq), tm=1024)

    return (yp.reshape(nb_p, seq, d), ys.reshape(nb_s, dseq, d),
            shp[None], scp[None],
            mk.reshape(1, nb_p, mem, xa_heads, xa_hd).astype(cache_mem_k.dtype),
            mv.reshape(1, nb_p, mem, xa_heads, xa_hd).astype(cache_mem_v.dtype),
            shs[None], scs[None])
```

```python
import functools

import numpy as np
import jax
import jax.numpy as jnp
from jax import lax
from jax.experimental import pallas as pl
from jax.experimental.pallas import tpu as pltpu

F32 = jnp.float32
BF16 = jnp.bfloat16
EPS = 1e-6
FFN_RES = 0.5
LANE = 128
ROWS = 128
CONV_W = 31
HIST = CONV_W - 1
VMEM_LIMIT = 56 * 1024 * 1024


def _params(*sem):
    return pltpu.CompilerParams(dimension_semantics=sem, vmem_limit_bytes=VMEM_LIMIT)


def _sigmoid(x):
    return 1.0 / (1.0 + jnp.exp(-x))


def _rms(x, g):
    return x * lax.rsqrt(jnp.mean(x * x, axis=-1, keepdims=True) + EPS) * g


def _dot(a, b):
    return jnp.dot(a, b, preferred_element_type=F32)


def _dot_nt(a, b):
    return lax.dot_general(a, b, (((1,), (1,)), ((), ())), preferred_element_type=F32)


def _dot_tn(a, b):
    return lax.dot_general(a, b, (((0,), (0,)), ((), ())), preferred_element_type=F32)


def _split3(x):
    hi = x.astype(BF16)
    r1 = x - hi.astype(F32)
    mid = r1.astype(BF16)
    lo = (r1 - mid.astype(F32)).astype(BF16)
    return hi, mid, lo


def _ffn_kernel(x_ref, g_ref, wg_ref, wu_ref, wd_ref, gf_ref, o_ref, *rest, final_norm, emit_bf16):
    j = pl.program_id(1)
    xn_ref = rest[-1]
    if emit_bf16:
        for src, dst in zip((wg_ref, wu_ref, wd_ref), rest[:3]):
            dst[...] = src[...].astype(BF16)
        wg_ref, wu_ref, wd_ref = rest[:3]

    @pl.when(j == 0)
    def _():
        xn_ref[...] = _rms(x_ref[...], g_ref[...]).astype(BF16)
        o_ref[...] = jnp.zeros_like(o_ref)

    xn = xn_ref[...]
    g = _dot(xn, wg_ref[...])
    u = _dot(xn, wu_ref[...])
    h = (g * _sigmoid(g) * u).astype(BF16)
    o_ref[...] += _dot(h, wd_ref[...])

    @pl.when(j == pl.num_programs(1) - 1)
    def _():
        y = x_ref[...] + FFN_RES * o_ref[...]
        if final_norm:
            y = _rms(y, gf_ref[...])
        o_ref[...] = y


def _ffn(x, g, wg, wu, wd, gf, *, final_norm, tm=512, tf=512):
    m, d = x.shape
    f = wg.shape[1]
    emit_bf16 = wg.dtype == F32
    if emit_bf16:
        tm = m
    assert m % tm == 0 and f % tf == 0
    row_mode = dict(pipeline_mode=pl.Buffered(1)) if m == tm else {}
    wspecs = [pl.BlockSpec((d, tf), lambda i, j: (0, j)),
              pl.BlockSpec((d, tf), lambda i, j: (0, j)),
              pl.BlockSpec((tf, d), lambda i, j: (j, 0))]
    out_specs = [pl.BlockSpec((tm, d), lambda i, j: (i, 0), **row_mode)]
    out_shape = [jax.ShapeDtypeStruct((m, d), F32)]
    if emit_bf16:
        out_specs += wspecs
        out_shape += [jax.ShapeDtypeStruct(w.shape, BF16) for w in (wg, wu, wd)]
    res = pl.pallas_call(
        functools.partial(_ffn_kernel, final_norm=final_norm, emit_bf16=emit_bf16),
        grid=(m // tm, f // tf),
        in_specs=[
            pl.BlockSpec((tm, d), lambda i, j: (i, 0), **row_mode),
            pl.BlockSpec((1, d), lambda i, j: (0, 0)),
            *wspecs,
            pl.BlockSpec((1, d), lambda i, j: (0, 0)),
        ],
        out_specs=out_specs,
        out_shape=out_shape,
        scratch_shapes=[pltpu.VMEM((tm, d), BF16)],
        compiler_params=_params("parallel", "arbitrary"),
        name="ffn_cast" if emit_bf16 else "ffn",
    )(x, g, wg, wu, wd, gf)
    return (res[0], tuple(res[1:])) if emit_bf16 else res[0]


def _norm_matmul_kernel(x_ref, g_ref, w_ref, o_ref, *rest, emit_bf16):
    xn_ref = rest[-1]
    if emit_bf16:
        rest[0][...] = w_ref[...].astype(BF16)
        w_ref = rest[0]

    @pl.when(pl.program_id(1) == 0)
    def _():
        xn_ref[...] = _rms(x_ref[...], g_ref[...]).astype(BF16)

    o_ref[...] = _dot(xn_ref[...], w_ref[...])


def _norm_matmul(x, g, w, *, tm, tn):
    m, d = x.shape
    n = w.shape[1]
    emit_bf16 = w.dtype == F32
    if emit_bf16:
        tm = m
    assert m % tm == 0 and n % tn == 0
    wspec = pl.BlockSpec((d, tn), lambda i, j: (0, j))
    out_specs = [pl.BlockSpec((tm, tn), lambda i, j: (i, j))]
    out_shape = [jax.ShapeDtypeStruct((m, n), F32)]
    if emit_bf16:
        out_specs.append(wspec)
        out_shape.append(jax.ShapeDtypeStruct(w.shape, BF16))
    res = pl.pallas_call(
        functools.partial(_norm_matmul_kernel, emit_bf16=emit_bf16),
        grid=(m // tm, n // tn),
        in_specs=[
            pl.BlockSpec((tm, d), lambda i, j: (i, 0)),
            pl.BlockSpec((1, d), lambda i, j: (0, 0)),
            wspec,
        ],
        out_specs=out_specs,
        out_shape=out_shape,
        scratch_shapes=[pltpu.VMEM((tm, d), BF16)],
        compiler_params=_params("parallel", "arbitrary"),
        name="norm_matmul_cast" if emit_bf16 else "norm_matmul",
    )(x, g, w)
    return tuple(res) if emit_bf16 else res[0]


def _outproj_kernel(x_ref, a_ref, b_ref, wa_ref, wb_ref, o_ref):
    o_ref[...] = x_ref[...] + _dot(a_ref[...], wa_ref[...]) + _dot(b_ref[...], wb_ref[...])


def _outproj(x, a, b, w, *, tm, tn):
    m, d = x.shape
    ka = a.shape[1]
    assert w.shape[0] == 2 * ka and b.shape[1] == ka and m % tm == 0 and d % tn == 0
    return pl.pallas_call(
        _outproj_kernel,
        grid=(m // tm, d // tn),
        in_specs=[
            pl.BlockSpec((tm, tn), lambda i, j: (i, j)),
            pl.BlockSpec((tm, ka), lambda i, j: (i, 0)),
            pl.BlockSpec((tm, ka), lambda i, j: (i, 0)),
            pl.BlockSpec((ka, tn), lambda i, j: (0, j)),
            pl.BlockSpec((ka, tn), lambda i, j: (1, j)),
        ],
        out_specs=pl.BlockSpec((tm, tn), lambda i, j: (i, j)),
        out_shape=jax.ShapeDtypeStruct((m, d), F32),
        compiler_params=_params("parallel", "arbitrary"),
        name="outproj",
    )(x, a, b, w, w)


def _hgrn_structure(bk):
    r = np.arange(ROWS)
    blk = r // bk
    tri = (blk[:, None] == blk[None, :]) & (r[None, :] <= r[:, None])
    masks = [np.eye(ROWS, dtype=bool)]
    m = 1
    while m < bk:
        grp = r // (2 * m)
        is_r = (r % (2 * m)) >= m
        masks.append((grp[:, None] == grp[None, :]) & is_r[:, None] & (~is_r)[None, :])
        m *= 2
    nblk = ROWS // bk
    rsel = blk[:, None] == (np.arange(nblk * LANE) // LANE)[None, :]
    return jnp.asarray(tri, BF16), jnp.asarray(np.stack(masks), F32), jnp.asarray(rsel, F32), nblk


def _hgrn_gates(zf_ref, lb_ref, tri_ref):
    lbr = lb_ref[...]
    e = jnp.exp(lbr - jnp.max(lbr, axis=0, keepdims=True))
    lb = e[0:1] / jnp.sum(e, axis=0, keepdims=True)
    f = lb + (1.0 - lb) * _sigmoid(zf_ref[...])
    parts = _split3(jnp.log2(f))
    tri = tri_ref[...]
    b = _dot(tri, parts[0]) + _dot(tri, parts[1]) + _dot(tri, parts[2])
    return f, parts, b


def _level_signs(row, bk):
    sgns = []
    m = 2
    while m < bk:
        sgns.append(jnp.where((row & (2 * m - 1)) >= m, 1.0, -1.0))
        m *= 2
    return sgns


def _level_weights(f, b, row, sgns, bk):
    ws = [jnp.where((row & 1) == 1, f, 1.0)]
    m = 2
    while m < bk:
        g = max(2 * m, 8)
        bg = b.reshape(ROWS // g, g, LANE)
        if 2 * m >= 8:
            ref = bg[:, m - 1:m, :]
        else:
            sub = row.reshape(ROWS // 8, 8, LANE) & 7
            ref = jnp.where(sub < 4, bg[:, 1:2, :], bg[:, 5:6, :])
        ws.append(jnp.exp2((bg - ref).reshape(ROWS, LANE) * sgns[len(ws) - 1]))
        m *= 2
    return ws


def _intra_scores(q, k, ws, mask_ref):
    a = mask_ref[0] * _dot_nt(q.astype(BF16), k.astype(BF16))
    for l, w in enumerate(ws):
        a = a + mask_ref[1 + l] * _dot_nt((q * w).astype(BF16), (k * w).astype(BF16))
    return a.astype(BF16)


def _hgrn_seq_kernel(zq_ref, zf_ref, zi_ref, zg_ref, lb_ref, gn_ref, tri_ref, mask_ref, s0_ref,
                     o_ref, sout_ref, st_scr, *, heads):
    t = pl.program_id(1)

    @pl.when(t == 0)
    def _():
        for h in range(heads):
            st_scr[h] = s0_ref[0, h].T

    f_all, _, b_all = _hgrn_gates(zf_ref, lb_ref, tri_ref)
    row = lax.broadcasted_iota(jnp.int32, (ROWS, LANE), 0)
    sgns = _level_signs(row, ROWS)
    for h in range(heads):
        sl = slice(h * LANE, (h + 1) * LANE)
        zq = zq_ref[:, sl]
        q = zq * _sigmoid(zq)
        f = f_all[:, sl]
        k = 1.0 - f
        b = b_all[:, sl]
        a = _intra_scores(q, k, _level_weights(f, b, row, sgns, ROWS), mask_ref)
        v16 = zi_ref[:, sl].astype(BF16)
        st = st_scr[h]
        bend = b[ROWS - 1:ROWS, :]
        o = _dot(a, v16) + _dot_nt((q * jnp.exp2(b)).astype(BF16), st.astype(BF16))
        kc = (k * jnp.exp2(bend - b)).astype(BF16)
        st_scr[h] = jnp.exp2(bend) * st + _dot_tn(v16, kc)
        zg = zg_ref[:, sl]
        o_ref[:, sl] = (_rms(o, gn_ref[...]) * (zg * _sigmoid(zg))).astype(o_ref.dtype)

    @pl.when(t == pl.num_programs(1) - 1)
    def _():
        for h in range(heads):
            sout_ref[0, h] = st_scr[h].T


def _hgrn_blocks_kernel(zq_ref, zf_ref, zi_ref, zg_ref, lb_ref, gn_ref, tri_ref, mask_ref, rsel_ref, s0_ref,
                        o_ref, sout_ref, *, heads, bk, nblk):
    f_all, parts, b_all = _hgrn_gates(zf_ref, lb_ref, tri_ref)
    row = lax.broadcasted_iota(jnp.int32, (ROWS, LANE), 0)
    sgns = _level_signs(row, bk)
    rsel = rsel_ref[...]
    rs16 = rsel.astype(BF16)
    for h in range(heads):
        sl = slice(h * LANE, (h + 1) * LANE)
        zq = zq_ref[:, sl]
        q = zq * _sigmoid(zq)
        f = f_all[:, sl]
        k = 1.0 - f
        b = b_all[:, sl]
        a = _intra_scores(q, k, _level_weights(f, b, row, sgns, bk), mask_ref)
        v = zi_ref[:, sl]
        scat = jnp.concatenate([s0_ref[i, h] for i in range(nblk)], axis=1)
        y = _dot((q * jnp.exp2(b)).astype(BF16), scat.astype(BF16))
        o = _dot(a, v.astype(BF16)) + jnp.concatenate(
            [y[i * bk:(i + 1) * bk, i * LANE:(i + 1) * LANE] for i in range(nblk)], axis=0)
        bg = b.reshape(nblk, bk, LANE)
        c = (bg[:, bk - 1:bk, :] - bg).reshape(ROWS, LANE)
        kc = (k * jnp.exp2(c)).astype(BF16)
        vblk = (jnp.concatenate([v] * nblk, axis=1) * rsel).astype(BF16)
        bend_t = (_dot_tn(parts[0][:, sl], rs16) + _dot_tn(parts[1][:, sl], rs16)
                  + _dot_tn(parts[2][:, sl], rs16))
        snew = jnp.exp2(bend_t) * scat + _dot_tn(kc, vblk)
        for i in range(nblk):
            sout_ref[i, h] = snew[:, i * LANE:(i + 1) * LANE]
        zg = zg_ref[:, sl]
        o_ref[:, sl] = (_rms(o, gn_ref[...]) * (zg * _sigmoid(zg))).astype(o_ref.dtype)


def _hgrn(z, lbp, gnorm, s0, *, bk, tiles, hs):
    m = z.shape[0]
    nseq, heads = s0.shape[0], s0.shape[1]
    tri, masks, rsel, nblk = _hgrn_structure(bk)
    groups = m // (ROWS * tiles)
    assert groups * nblk == nseq and groups * tiles * ROWS == m and heads % hs == 0
    hb = heads // hs
    w = hs * LANE
    seq_mode = nblk == 1

    def zspec(part):
        if seq_mode:
            return pl.BlockSpec((ROWS, w), lambda g, j: (g * tiles + j, part))
        return pl.BlockSpec((ROWS, w), lambda g, j: (g, part * hb + j))

    const2 = lambda g, j: (0, 0)
    const3 = lambda g, j: (0, 0, 0)
    common = [zspec(0), zspec(1), zspec(2), zspec(3),
              pl.BlockSpec((lbp.shape[0], w), const2 if seq_mode else (lambda g, j: (0, j))),
              pl.BlockSpec((1, LANE), const2),
              pl.BlockSpec(tri.shape, const2),
              pl.BlockSpec(masks.shape, const3)]
    out_shape = [jax.ShapeDtypeStruct((m, heads * LANE), BF16), jax.ShapeDtypeStruct(s0.shape, s0.dtype)]
    if seq_mode:
        assert hs == heads
        sspec = pl.BlockSpec((1, heads, LANE, LANE), lambda g, j: (g, 0, 0, 0))
        return pl.pallas_call(
            functools.partial(_hgrn_seq_kernel, heads=heads),
            grid=(groups, tiles),
            in_specs=common + [sspec],
            out_specs=[pl.BlockSpec((ROWS, w), lambda g, j: (g * tiles + j, 0)), sspec],
            out_shape=out_shape,
            scratch_shapes=[pltpu.VMEM((heads, LANE, LANE), F32)],
            compiler_params=_params("parallel", "arbitrary"),
            name="hgrn_seq",
        )(z, z, z, z, lbp, gnorm, tri, masks, s0)
    assert tiles == 1
    sspec = pl.BlockSpec((nblk, hs, LANE, LANE), lambda g, j: (g, j, 0, 0))
    return pl.pallas_call(
        functools.partial(_hgrn_blocks_kernel, heads=hs, bk=bk, nblk=nblk),
        grid=(groups, hb),
        in_specs=common + [pl.BlockSpec(rsel.shape, const2), sspec],
        out_specs=[pl.BlockSpec((ROWS, w), lambda g, j: (g, j)), sspec],
        out_shape=out_shape,
        compiler_params=_params("parallel", "parallel"),
        name="hgrn_blocks",
    )(z, z, z, z, lbp, gnorm, tri, masks, rsel, s0)


def _conv_kernel(za_ref, zb_ref, buf_ref, w_ref, cb_ref, lg_ref, lb_ref, o_ref, bufout_ref, ucat, dw,
                 *, ns, ls, nc):
    u = za_ref[...] * _sigmoid(zb_ref[...])

    @pl.when(pl.program_id(1) == 0)
    def _():
        for i in range(ns):
            for cb in range(nc):
                ucat[i, cb, 0:HIST, :] = buf_ref[i, :, cb * LANE:(cb + 1) * LANE]

    for i in range(ns):
        for cb in range(nc):
            ucat[i, cb, HIST:HIST + ls, :] = u[i * ls:(i + 1) * ls, cb * LANE:(cb + 1) * LANE]

    rt = min(ls, 64)

    def cb_body(cb, carry):
        for i in range(ns):
            for r0 in range(0, ls, rt):
                acc = jnp.broadcast_to(cb_ref[cb], (rt, LANE))
                for j in range(CONV_W):
                    acc = acc + ucat[i, cb, r0 + j:r0 + j + rt, :] * w_ref[cb, j:j + 1, :]
                dw[i, cb, r0:r0 + rt, :] = acc
        return carry

    lax.fori_loop(0, nc, cb_body, 0)

    rt2 = min(ls, 32)
    inv_c = 1.0 / (nc * LANE)
    for i in range(ns):
        for r0 in range(0, ls, rt2):
            x = dw[i, :, r0:r0 + rt2, :]
            mu = jnp.sum(jnp.sum(x, axis=0), axis=-1, keepdims=True) * inv_c
            xc = x - mu[None]
            var = jnp.sum(jnp.sum(xc * xc, axis=0), axis=-1, keepdims=True) * inv_c
            y = xc * lax.rsqrt(var + EPS)[None] * lg_ref[...] + lb_ref[...]
            y = y * _sigmoid(y)
            for cb in range(nc):
                o_ref[i * ls + r0:i * ls + r0 + rt2, cb * LANE:(cb + 1) * LANE] = y[cb].astype(o_ref.dtype)

    for i in range(ns):
        for cb in range(nc):
            tail = ucat[i, cb, ls:ls + HIST, :]
            bufout_ref[i, :, cb * LANE:(cb + 1) * LANE] = tail
            ucat[i, cb, 0:HIST, :] = tail


def _conv(z, buf0, w3, cb3, lg3, lb3, *, ns, ls, tiles, za_col):
    m = z.shape[0]
    nseq, _, c = buf0.shape
    nc = c // LANE
    groups = nseq // ns
    assert groups * tiles * ns * ls == m
    rows = ns * ls
    pad_rows = -(-(HIST + ls) // 8) * 8
    bspec = pl.BlockSpec((ns, HIST, c), lambda g, t: (g, 0, 0))
    const3 = lambda g, t: (0, 0, 0)
    return pl.pallas_call(
        functools.partial(_conv_kernel, ns=ns, ls=ls, nc=nc),
        grid=(groups, tiles),
        in_specs=[
            pl.BlockSpec((rows, c), lambda g, t: (g * tiles + t, za_col)),
            pl.BlockSpec((rows, c), lambda g, t: (g * tiles + t, za_col + 1)),
            bspec,
            pl.BlockSpec(w3.shape, const3),
            pl.BlockSpec(cb3.shape, const3),
            pl.BlockSpec(lg3.shape, const3),
            pl.BlockSpec(lb3.shape, const3),
        ],
        out_specs=[pl.BlockSpec((rows, c), lambda g, t: (g * tiles + t, 0)), bspec],
        out_shape=[jax.ShapeDtypeStruct((m, c), BF16), jax.ShapeDtypeStruct(buf0.shape, buf0.dtype)],
        scratch_shapes=[pltpu.VMEM((ns, nc, pad_rows, LANE), F32), pltpu.VMEM((ns, nc, ls, LANE), F32)],
        compiler_params=_params("parallel", "arbitrary"),
        name="conv",
    )(z, z, buf0, w3, cb3, lg3, lb3)


def _xattn_kernel(x_ref, g_ref, wq_ref, k_ref, v_ref, wo_ref, o_ref, *, nb, lq, heads, interleaved):
    x = x_ref[...]
    q = _dot(_rms(x, g_ref[...]).astype(BF16), wq_ref[...])
    hd = q.shape[1] // heads
    kv_rows = k_ref.shape[0] // heads if interleaved else k_ref.shape[0]
    mem = kv_rows // nb
    scale = hd ** -0.5
    if nb > 1:
        rows = lax.broadcasted_iota(jnp.int32, (nb * lq, nb * mem), 0) // lq
        cols = lax.broadcasted_iota(jnp.int32, (nb * lq, nb * mem), 1) // mem
        own = rows == cols
    outs = []
    for h in range(heads):
        sl = slice(h * hd, (h + 1) * hd)
        if interleaved:
            kh = k_ref[pl.ds(h, kv_rows, stride=heads), :].astype(BF16)
            vh = v_ref[pl.ds(h, kv_rows, stride=heads), :].astype(BF16)
        else:
            kh = k_ref[:, sl].astype(BF16)
            vh = v_ref[:, sl].astype(BF16)
        s = _dot_nt(q[:, sl].astype(BF16), kh) * scale
        if nb > 1:
            s = jnp.where(own, s, -1e30)
        p = jnp.exp(s - jnp.max(s, axis=-1, keepdims=True))
        l = jnp.sum(p, axis=-1, keepdims=True)
        outs.append((_dot(p.astype(BF16), vh) / l).astype(BF16))
    o_ref[...] = x + _dot(jnp.concatenate(outs, axis=1), wo_ref[...])


def _xattn(x, g, wq, k2d, v2d, wo, *, nb, lq, mem, heads):
    m, d = x.shape
    xa = wq.shape[1]
    interleaved = k2d.shape[1] != xa
    kv_blk = (nb * mem * heads, xa // heads) if interleaved else (nb * mem, xa)
    rows = nb * lq
    assert m % rows == 0
    steps = m // rows
    nseq = k2d.shape[0] * nb // kv_blk[0]
    tiles = steps * nb // nseq
    assert tiles * nseq == steps * nb and (nb == 1 or tiles == 1)
    kv_map = lambda i: (i // tiles, 0)
    const = lambda i: (0, 0)
    return pl.pallas_call(
        functools.partial(_xattn_kernel, nb=nb, lq=lq, heads=heads, interleaved=interleaved),
        grid=(steps,),
        in_specs=[
            pl.BlockSpec((rows, d), lambda i: (i, 0)),
            pl.BlockSpec((1, d), const),
            pl.BlockSpec((d, xa), const),
            pl.BlockSpec(kv_blk, kv_map),
            pl.BlockSpec(kv_blk, kv_map),
            pl.BlockSpec((xa, d), const),
        ],
        out_specs=pl.BlockSpec((rows, d), lambda i: (i, 0)),
        out_shape=jax.ShapeDtypeStruct((m, d), F32),
        compiler_params=_params("parallel"),
        name="xattn",
    )(x, g, wq, k2d, v2d, wo)


def kernel(x_prompt, x_sample, mem_prompt, state_hgrn, state_conv, cache_mem_k, cache_mem_v, norm_ffn1, ffn1_w_gate, ffn1_w_up, ffn1_w_down, norm_mix, w_in, hgrn_lb, hgrn_gnorm, conv_w, conv_b, conv_ln_g, conv_ln_b, w_out, norm_xattn, norm_mem, xattn_wq, xattn_wk, xattn_wv, xattn_wo, norm_ffn2, ffn2_w_gate, ffn2_w_up, ffn2_w_down, norm_final):
    depth = norm_ffn1.shape[0]
    assert depth == 1
    nb_p, seq, d = x_prompt.shape
    nb_s, dseq, _ = x_sample.shape
    mem = mem_prompt.shape[1]
    heads_a, expand, vdim = state_hgrn.shape[2:]
    c_conv = state_conv.shape[-1]
    xa_heads, xa_hd = cache_mem_k.shape[3:]
    xa = xa_heads * xa_hd
    assert expand == LANE and vdim == LANE and c_conv == heads_a * LANE
    za_col = (2 * heads_a * expand + 2 * heads_a * vdim) // c_conv

    l = 0
    row = lambda p: p.reshape(1, -1).astype(F32)
    bf = lambda w: w[l].astype(BF16)
    w_out16 = bf(w_out)
    wq16, wk16, wv16, wo16 = bf(xattn_wq), bf(xattn_wk), bf(xattn_wv), bf(xattn_wo)
    n_mix, n_xa, n_mem, n_fin = row(norm_mix[l]), row(norm_xattn[l]), row(norm_mem[l]), row(norm_final)
    lbp = hgrn_lb.astype(F32)
    gnorm = row(hgrn_gnorm[l])
    nc = c_conv // LANE
    chan3 = lambda p: p.reshape(-1, nc, LANE).transpose(1, 0, 2).astype(F32)
    w3, cb3, lg3, lb3 = chan3(conv_w[l]), chan3(conv_b[l]), chan3(conv_ln_g[l]), chan3(conv_ln_b[l])
    ones_d = jnp.ones((1, d), F32)

    def block(x, k2d, v2d, s0, buf0, f1w, w_in_x, f2w, *, hg, cv, xat, tm, tf, tn):
        x = _ffn(x, row(norm_ffn1[l]), *f1w, ones_d, final_norm=False, tf=tf)
        if f1w[0].dtype == F32:
            x, f1w = x
        z = _norm_matmul(x, n_mix, w_in_x, tm=tm, tn=tn)
        if w_in_x.dtype == F32:
            z, w_in_x = z
        o_a, s_new = _hgrn(z, lbp, gnorm, s0, **hg)
        o_b, buf_new = _conv(z, buf0, w3, cb3, lg3, lb3, za_col=za_col, **cv)
        x = _outproj(x, o_a, o_b, w_out16, tm=512, tn=d)
        x = _xattn(x, n_xa, wq16, k2d, v2d, wo16, mem=mem, heads=xa_heads, **xat)
        y = _ffn(x, row(norm_ffn2[l]), *f2w, n_fin, final_norm=True, tf=tf)
        if f2w[0].dtype == F32:
            y, f2w = y
        return y, s_new, buf_new, (f1w, w_in_x, f2w)

    ys, shs, scs, w16 = block(
        x_sample.reshape(nb_s * dseq, d),
        cache_mem_k[l].reshape(nb_s * mem * xa_heads, xa_hd),
        cache_mem_v[l].reshape(nb_s * mem * xa_heads, xa_hd),
        state_hgrn[l], state_conv[l],
        (ffn1_w_gate[l], ffn1_w_up[l], ffn1_w_down[l]), w_in[l],
        (ffn2_w_gate[l], ffn2_w_up[l], ffn2_w_down[l]),
        hg=dict(bk=dseq, tiles=1, hs=4),
        cv=dict(ns=8, ls=dseq, tiles=1),
        xat=dict(nb=8, lq=dseq), tm=1024, tf=256, tn=512)

    memx = mem_prompt.reshape(nb_p * mem, d)
    mk = _norm_matmul(memx, n_mem, wk16, tm=512, tn=xa)
    mv = _norm_matmul(memx, n_mem, wv16, tm=512, tn=xa)
    s0p = jnp.zeros((nb_p, heads_a, expand, vdim), state_hgrn.dtype)
    b0p = jnp.zeros((nb_p, HIST, c_conv), state_conv.dtype)
    yp, shp, scp, _ = block(
        x_prompt.reshape(nb_p * seq, d), mk, mv, s0p, b0p, *w16,
        hg=dict(bk=ROWS, tiles=seq // ROWS, hs=heads_a),
        cv=dict(ns=1, ls=256, tiles=seq // 256),
        xat=dict(nb=1, lq=512), tm=1024, tf=512, tn=1024)

    return (yp.reshape(nb_p, seq, d), ys.reshape(nb_s, dseq, d),
            shp[None], scp[None],
            mk.reshape(1, nb_p, mem, xa_heads, xa_hd).astype(cache_mem_k.dtype),
            mv.reshape(1, nb_p, mem, xa_heads, xa_hd).astype(cache_mem_v.dtype),
            shs[None], scs[None])
```

```python
import functools

import numpy as np
import jax
import jax.numpy as jnp
from jax import lax
from jax.experimental import pallas as pl
from jax.experimental.pallas import tpu as pltpu

F32 = jnp.float32
BF16 = jnp.bfloat16
EPS = 1e-6
FFN_RES = 0.5
LANE = 128
ROWS = 128
CONV_W = 31
HIST = CONV_W - 1
VMEM_LIMIT = 56 * 1024 * 1024


def _params(*sem):
    return pltpu.CompilerParams(dimension_semantics=sem, vmem_limit_bytes=VMEM_LIMIT)


def _sigmoid(x):
    return 1.0 / (1.0 + jnp.exp(-x))


def _rms(x, g):
    return x * lax.rsqrt(jnp.mean(x * x, axis=-1, keepdims=True) + EPS) * g


def _dot(a, b):
    return jnp.dot(a, b, preferred_element_type=F32)


def _dot_nt(a, b):
    return lax.dot_general(a, b, (((1,), (1,)), ((), ())), preferred_element_type=F32)


def _dot_tn(a, b):
    return lax.dot_general(a, b, (((0,), (0,)), ((), ())), preferred_element_type=F32)


def _split3(x):
    hi = x.astype(BF16)
    r1 = x - hi.astype(F32)
    mid = r1.astype(BF16)
    lo = (r1 - mid.astype(F32)).astype(BF16)
    return hi, mid, lo


def _ffn_kernel(x_ref, g_ref, wg_ref, wu_ref, wd_ref, gf_ref, o_ref, *rest, final_norm, emit_bf16):
    j = pl.program_id(1)
    xn_ref = rest[-1]
    if emit_bf16:
        for src, dst in zip((wg_ref, wu_ref, wd_ref), rest[:3]):
            dst[...] = src[...].astype(BF16)
        wg_ref, wu_ref, wd_ref = rest[:3]

    slab = 128
    nslab = x_ref.shape[0] // slab

    @pl.when(j == 0)
    def _():
        def body(c, carry):
            r = pl.ds(pl.multiple_of(c * slab, slab), slab)
            xn_ref[r, :] = _rms(x_ref[r, :], g_ref[...]).astype(BF16)
            o_ref[r, :] = jnp.zeros((slab, o_ref.shape[1]), o_ref.dtype)
            return carry
        lax.fori_loop(0, nslab, body, 0)

    xn = xn_ref[...]
    g = _dot(xn, wg_ref[...])
    u = _dot(xn, wu_ref[...])
    h = (g * _sigmoid(g) * u).astype(BF16)
    o_ref[...] += _dot(h, wd_ref[...])

    @pl.when(j == pl.num_programs(1) - 1)
    def _():
        def body(c, carry):
            r = pl.ds(pl.multiple_of(c * slab, slab), slab)
            y = x_ref[r, :] + FFN_RES * o_ref[r, :]
            if final_norm:
                y = _rms(y, gf_ref[...])
            o_ref[r, :] = y
            return carry
        lax.fori_loop(0, nslab, body, 0)


def _ffn(x, g, wg, wu, wd, gf, *, final_norm, tm=1024, tf=512):
    m, d = x.shape
    f = wg.shape[1]
    emit_bf16 = wg.dtype == F32
    if emit_bf16:
        tm = m
    assert m % tm == 0 and f % tf == 0
    row_mode = dict(pipeline_mode=pl.Buffered(1)) if m == tm else {}
    wspecs = [pl.BlockSpec((d, tf), lambda i, j: (0, j)),
              pl.BlockSpec((d, tf), lambda i, j: (0, j)),
              pl.BlockSpec((tf, d), lambda i, j: (j, 0))]
    out_specs = [pl.BlockSpec((tm, d), lambda i, j: (i, 0), **row_mode)]
    out_shape = [jax.ShapeDtypeStruct((m, d), F32)]
    if emit_bf16:
        out_specs += wspecs
        out_shape += [jax.ShapeDtypeStruct(w.shape, BF16) for w in (wg, wu, wd)]
    res = pl.pallas_call(
        functools.partial(_ffn_kernel, final_norm=final_norm, emit_bf16=emit_bf16),
        grid=(m // tm, f // tf),
        in_specs=[
            pl.BlockSpec((tm, d), lambda i, j: (i, 0), **row_mode),
            pl.BlockSpec((1, d), lambda i, j: (0, 0)),
            *wspecs,
            pl.BlockSpec((1, d), lambda i, j: (0, 0)),
        ],
        out_specs=out_specs,
        out_shape=out_shape,
        scratch_shapes=[pltpu.VMEM((tm, d), BF16)],
        compiler_params=_params("parallel", "arbitrary"),
        name="ffn_cast" if emit_bf16 else "ffn",
    )(x, g, wg, wu, wd, gf)
    return (res[0], tuple(res[1:])) if emit_bf16 else res[0]


def _norm_matmul_kernel(x_ref, g_ref, w_ref, o_ref, *rest, emit_bf16):
    xn_ref = rest[-1]
    if emit_bf16:
        rest[0][...] = w_ref[...].astype(BF16)
        w_ref = rest[0]

    @pl.when(pl.program_id(1) == 0)
    def _():
        xn_ref[...] = _rms(x_ref[...], g_ref[...]).astype(BF16)

    o_ref[...] = _dot(xn_ref[...], w_ref[...])


def _norm_matmul(x, g, w, *, tm, tn):
    m, d = x.shape
    n = w.shape[1]
    emit_bf16 = w.dtype == F32
    if emit_bf16:
        tm = m
    assert m % tm == 0 and n % tn == 0
    wspec = pl.BlockSpec((d, tn), lambda i, j: (0, j))
    out_specs = [pl.BlockSpec((tm, tn), lambda i, j: (i, j))]
    out_shape = [jax.ShapeDtypeStruct((m, n), F32)]
    if emit_bf16:
        out_specs.append(wspec)
        out_shape.append(jax.ShapeDtypeStruct(w.shape, BF16))
    res = pl.pallas_call(
        functools.partial(_norm_matmul_kernel, emit_bf16=emit_bf16),
        grid=(m // tm, n // tn),
        in_specs=[
            pl.BlockSpec((tm, d), lambda i, j: (i, 0)),
            pl.BlockSpec((1, d), lambda i, j: (0, 0)),
            wspec,
        ],
        out_specs=out_specs,
        out_shape=out_shape,
        scratch_shapes=[pltpu.VMEM((tm, d), BF16)],
        compiler_params=_params("parallel", "arbitrary"),
        name="norm_matmul_cast" if emit_bf16 else "norm_matmul",
    )(x, g, w)
    return tuple(res) if emit_bf16 else res[0]


def _outproj_kernel(x_ref, a_ref, b_ref, wa_ref, wb_ref, o_ref):
    o_ref[...] = x_ref[...] + _dot(a_ref[...], wa_ref[...]) + _dot(b_ref[...], wb_ref[...])


def _outproj(x, a, b, w, *, tm, tn):
    m, d = x.shape
    ka = a.shape[1]
    assert w.shape[0] == 2 * ka and b.shape[1] == ka and m % tm == 0 and d % tn == 0
    return pl.pallas_call(
        _outproj_kernel,
        grid=(m // tm, d // tn),
        in_specs=[
            pl.BlockSpec((tm, tn), lambda i, j: (i, j)),
            pl.BlockSpec((tm, ka), lambda i, j: (i, 0)),
            pl.BlockSpec((tm, ka), lambda i, j: (i, 0)),
            pl.BlockSpec((ka, tn), lambda i, j: (0, j)),
            pl.BlockSpec((ka, tn), lambda i, j: (1, j)),
        ],
        out_specs=pl.BlockSpec((tm, tn), lambda i, j: (i, j)),
        out_shape=jax.ShapeDtypeStruct((m, d), F32),
        compiler_params=_params("parallel", "arbitrary"),
        name="outproj",
    )(x, a, b, w, w)


def _hgrn_structure(bk):
    r = np.arange(ROWS)
    blk = r // bk
    tri = (blk[:, None] == blk[None, :]) & (r[None, :] <= r[:, None])
    masks = [np.eye(ROWS, dtype=bool)]
    m = 1
    while m < bk:
        grp = r // (2 * m)
        is_r = (r % (2 * m)) >= m
        masks.append((grp[:, None] == grp[None, :]) & is_r[:, None] & (~is_r)[None, :])
        m *= 2
    nblk = ROWS // bk
    rsel = blk[:, None] == (np.arange(nblk * LANE) // LANE)[None, :]
    return jnp.asarray(tri, BF16), jnp.asarray(np.stack(masks), F32), jnp.asarray(rsel, F32), nblk


def _hgrn_gates(zf_ref, lb_ref, tri_ref):
    lbr = lb_ref[...]
    e = jnp.exp(lbr - jnp.max(lbr, axis=0, keepdims=True))
    lb = e[0:1] / jnp.sum(e, axis=0, keepdims=True)
    f = lb + (1.0 - lb) * _sigmoid(zf_ref[...])
    parts = _split3(jnp.log2(f))
    tri = tri_ref[...]
    b = _dot(tri, parts[0]) + _dot(tri, parts[1]) + _dot(tri, parts[2])
    return f, parts, b


def _level_signs(row, bk):
    sgns = []
    m = 2
    while m < bk:
        sgns.append(jnp.where((row & (2 * m - 1)) >= m, 1.0, -1.0))
        m *= 2
    return sgns


def _level_weights(f, b, row, sgns, bk):
    width = b.shape[1]
    ws = [jnp.where((row & 1) == 1, f, 1.0)]
    m = 2
    while m < bk:
        g = max(2 * m, 8)
        bg = b.reshape(ROWS // g, g, width)
        if 2 * m >= 8:
            ref = bg[:, m - 1:m, :]
        else:
            sub = row.reshape(ROWS // 8, 8, width) & 7
            ref = jnp.where(sub < 4, bg[:, 1:2, :], bg[:, 5:6, :])
        ws.append(jnp.exp2((bg - ref).reshape(ROWS, width) * sgns[len(ws) - 1]))
        m *= 2
    return ws


def _intra_scores(q, k, ws, mask_ref):
    a = mask_ref[0] * _dot_nt(q.astype(BF16), k.astype(BF16))
    for l, w in enumerate(ws):
        a = a + mask_ref[1 + l] * _dot_nt((q * w).astype(BF16), (k * w).astype(BF16))
    return a.astype(BF16)


def _hgrn_seq_kernel(zq_ref, zf_ref, zi_ref, zg_ref, lb_ref, gn_ref, tri_ref, mask_ref, s0_ref,
                     o_ref, sout_ref, st_scr, *, heads):
    t = pl.program_id(1)

    @pl.when(t == 0)
    def _():
        for h in range(heads):
            st_scr[h] = s0_ref[0, h].T

    row = lax.broadcasted_iota(jnp.int32, (ROWS, LANE), 0)
    sgns = _level_signs(row, ROWS)
    for tt in range(zq_ref.shape[0] // ROWS):
        rs = slice(tt * ROWS, (tt + 1) * ROWS)
        f_all, _, b_all = _hgrn_gates(zf_ref.at[rs, :], lb_ref, tri_ref)
        for h in range(heads):
            sl = slice(h * LANE, (h + 1) * LANE)
            zq = zq_ref[rs, sl]
            q = zq * _sigmoid(zq)
            f = f_all[:, sl]
            k = 1.0 - f
            b = b_all[:, sl]
            a = _intra_scores(q, k, _level_weights(f, b, row, sgns, ROWS), mask_ref)
            v16 = zi_ref[rs, sl].astype(BF16)
            st = st_scr[h]
            bend = b[ROWS - 1:ROWS, :]
            o = _dot(a, v16) + _dot_nt((q * jnp.exp2(b)).astype(BF16), st.astype(BF16))
            kc = (k * jnp.exp2(bend - b)).astype(BF16)
            st_scr[h] = jnp.exp2(bend) * st + _dot_tn(v16, kc)
            zg = zg_ref[rs, sl]
            o_ref[rs, sl] = (_rms(o, gn_ref[...]) * (zg * _sigmoid(zg))).astype(o_ref.dtype)

    @pl.when(t == pl.num_programs(1) - 1)
    def _():
        for h in range(heads):
            sout_ref[0, h] = st_scr[h].T


def _hgrn_blocks_kernel(zq_ref, zf_ref, zi_ref, zg_ref, lb_ref, gn_ref, tri_ref, mask_ref, rsel_ref, s0_ref,
                        o_ref, sout_ref, *, heads, bk, nblk):
    f_all, parts, b_all = _hgrn_gates(zf_ref, lb_ref, tri_ref)
    row = lax.broadcasted_iota(jnp.int32, (ROWS, LANE), 0)
    sgns = _level_signs(row, bk)
    rsel = rsel_ref[...]
    rs16 = rsel.astype(BF16)
    for h in range(heads):
        sl = slice(h * LANE, (h + 1) * LANE)
        zq = zq_ref[:, sl]
        q = zq * _sigmoid(zq)
        f = f_all[:, sl]
        k = 1.0 - f
        b = b_all[:, sl]
        a = _intra_scores(q, k, _level_weights(f, b, row, sgns, bk), mask_ref)
        v = zi_ref[:, sl]
        scat = jnp.concatenate([s0_ref[i, h] for i in range(nblk)], axis=1)
        y = _dot((q * jnp.exp2(b)).astype(BF16), scat.astype(BF16))
        o = _dot(a, v.astype(BF16)) + jnp.concatenate(
            [y[i * bk:(i + 1) * bk, i * LANE:(i + 1) * LANE] for i in range(nblk)], axis=0)
        bg = b.reshape(nblk, bk, LANE)
        c = (bg[:, bk - 1:bk, :] - bg).reshape(ROWS, LANE)
        kc = (k * jnp.exp2(c)).astype(BF16)
        vblk = (jnp.concatenate([v] * nblk, axis=1) * rsel).astype(BF16)
        bend_t = (_dot_tn(parts[0][:, sl], rs16) + _dot_tn(parts[1][:, sl], rs16)
                  + _dot_tn(parts[2][:, sl], rs16))
        snew = jnp.exp2(bend_t) * scat + _dot_tn(kc, vblk)
        for i in range(nblk):
            sout_ref[i, h] = snew[:, i * LANE:(i + 1) * LANE]
        zg = zg_ref[:, sl]
        o_ref[:, sl] = (_rms(o, gn_ref[...]) * (zg * _sigmoid(zg))).astype(o_ref.dtype)


def _hgrn(z, lbp, gnorm, s0, *, bk, tiles, hs, tps=1):
    m = z.shape[0]
    nseq, heads = s0.shape[0], s0.shape[1]
    tri, masks, rsel, nblk = _hgrn_structure(bk)
    groups = m // (ROWS * tiles)
    assert groups * nblk == nseq and groups * tiles * ROWS == m and heads % hs == 0 and tiles % tps == 0
    steps = tiles // tps
    hb = heads // hs
    w = hs * LANE
    seq_mode = nblk == 1

    def zspec(part):
        if seq_mode:
            return pl.BlockSpec((tps * ROWS, w), lambda g, j: (g * steps + j, part))
        return pl.BlockSpec((ROWS, w), lambda g, j: (g, part * hb + j))

    const2 = lambda g, j: (0, 0)
    const3 = lambda g, j: (0, 0, 0)
    common = [zspec(0), zspec(1), zspec(2), zspec(3),
              pl.BlockSpec((lbp.shape[0], w), const2 if seq_mode else (lambda g, j: (0, j))),
              pl.BlockSpec((1, LANE), const2),
              pl.BlockSpec(tri.shape, const2),
              pl.BlockSpec(masks.shape, const3)]
    out_shape = [jax.ShapeDtypeStruct((m, heads * LANE), BF16), jax.ShapeDtypeStruct(s0.shape, s0.dtype)]
    if seq_mode:
        assert hs == heads
        sspec = pl.BlockSpec((1, heads, LANE, LANE), lambda g, j: (g, 0, 0, 0))
        return pl.pallas_call(
            functools.partial(_hgrn_seq_kernel, heads=heads),
            grid=(groups, steps),
            in_specs=common + [sspec],
            out_specs=[pl.BlockSpec((tps * ROWS, w), lambda g, j: (g * steps + j, 0)), sspec],
            out_shape=out_shape,
            scratch_shapes=[pltpu.VMEM((heads, LANE, LANE), F32)],
            compiler_params=_params("parallel", "arbitrary"),
            name="hgrn_seq",
        )(z, z, z, z, lbp, gnorm, tri, masks, s0)
    assert tiles == 1
    sspec = pl.BlockSpec((nblk, hs, LANE, LANE), lambda g, j: (g, j, 0, 0))
    return pl.pallas_call(
        functools.partial(_hgrn_blocks_kernel, heads=hs, bk=bk, nblk=nblk),
        grid=(groups, hb),
        in_specs=common + [pl.BlockSpec(rsel.shape, const2), sspec],
        out_specs=[pl.BlockSpec((ROWS, w), lambda g, j: (g, j)), sspec],
        out_shape=out_shape,
        compiler_params=_params("parallel", "parallel"),
        name="hgrn_blocks",
    )(z, z, z, z, lbp, gnorm, tri, masks, rsel, s0)


def _conv_kernel(za_ref, zb_ref, buf_ref, w_ref, cb_ref, lg_ref, lb_ref, o_ref, bufout_ref, ucat, dw,
                 *, ns, ls, nc):
    u = za_ref[...] * _sigmoid(zb_ref[...])

    @pl.when(pl.program_id(1) == 0)
    def _():
        for i in range(ns):
            for cb in range(nc):
                ucat[i, cb, 0:HIST, :] = buf_ref[i, :, cb * LANE:(cb + 1) * LANE]

    for i in range(ns):
        for cb in range(nc):
            ucat[i, cb, HIST:HIST + ls, :] = u[i * ls:(i + 1) * ls, cb * LANE:(cb + 1) * LANE]

    rt = min(ls, 64)

    def cb_body(cb, carry):
        for i in range(ns):
            for r0 in range(0, ls, rt):
                acc = jnp.broadcast_to(cb_ref[cb], (rt, LANE))
                for j in range(CONV_W):
                    acc = acc + ucat[i, cb, r0 + j:r0 + j + rt, :] * w_ref[cb, j:j + 1, :]
                dw[i, cb, r0:r0 + rt, :] = acc
        return carry

    lax.fori_loop(0, nc, cb_body, 0)

    rt2 = min(ls, 32)
    inv_c = 1.0 / (nc * LANE)
    for i in range(ns):
        for r0 in range(0, ls, rt2):
            x = dw[i, :, r0:r0 + rt2, :]
            mu = jnp.sum(jnp.sum(x, axis=0), axis=-1, keepdims=True) * inv_c
            xc = x - mu[None]
            var = jnp.sum(jnp.sum(xc * xc, axis=0), axis=-1, keepdims=True) * inv_c
            y = xc * lax.rsqrt(var + EPS)[None] * lg_ref[...] + lb_ref[...]
            y = y * _sigmoid(y)
            for cb in range(nc):
                o_ref[i * ls + r0:i * ls + r0 + rt2, cb * LANE:(cb + 1) * LANE] = y[cb].astype(o_ref.dtype)

    for i in range(ns):
        for cb in range(nc):
            tail = ucat[i, cb, ls:ls + HIST, :]
            bufout_ref[i, :, cb * LANE:(cb + 1) * LANE] = tail
            ucat[i, cb, 0:HIST, :] = tail


def _conv(z, buf0, w3, cb3, lg3, lb3, *, ns, ls, tiles, za_col):
    m = z.shape[0]
    nseq, _, c = buf0.shape
    nc = c // LANE
    groups = nseq // ns
    assert groups * tiles * ns * ls == m
    rows = ns * ls
    pad_rows = -(-(HIST + ls) // 8) * 8
    bspec = pl.BlockSpec((ns, HIST, c), lambda g, t: (g, 0, 0))
    const3 = lambda g, t: (0, 0, 0)
    return pl.pallas_call(
        functools.partial(_conv_kernel, ns=ns, ls=ls, nc=nc),
        grid=(groups, tiles),
        in_specs=[
            pl.BlockSpec((rows, c), lambda g, t: (g * tiles + t, za_col)),
            pl.BlockSpec((rows, c), lambda g, t: (g * tiles + t, za_col + 1)),
            bspec,
            pl.BlockSpec(w3.shape, const3),
            pl.BlockSpec(cb3.shape, const3),
            pl.BlockSpec(lg3.shape, const3),
            pl.BlockSpec(lb3.shape, const3),
        ],
        out_specs=[pl.BlockSpec((rows, c), lambda g, t: (g * tiles + t, 0)), bspec],
        out_shape=[jax.ShapeDtypeStruct((m, c), BF16), jax.ShapeDtypeStruct(buf0.shape, buf0.dtype)],
        scratch_shapes=[pltpu.VMEM((ns, nc, pad_rows, LANE), F32), pltpu.VMEM((ns, nc, ls, LANE), F32)],
        compiler_params=_params("parallel", "arbitrary"),
        name="conv",
    )(z, z, buf0, w3, cb3, lg3, lb3)


def _xattn_kernel(x_ref, g_ref, wq_ref, k_ref, v_ref, wo_ref, o_ref, *, nb, lq, heads, interleaved):
    x = x_ref[...]
    q = _dot(_rms(x, g_ref[...]).astype(BF16), wq_ref[...])
    hd = q.shape[1] // heads
    kv_rows = k_ref.shape[0] // heads if interleaved else k_ref.shape[0]
    mem = kv_rows // nb
    scale = hd ** -0.5
    if nb > 1:
        rows = lax.broadcasted_iota(jnp.int32, (nb * lq, nb * mem), 0) // lq
        cols = lax.broadcasted_iota(jnp.int32, (nb * lq, nb * mem), 1) // mem
        own = rows == cols
    outs = []
    for h in range(heads):
        sl = slice(h * hd, (h + 1) * hd)
        if interleaved:
            kh = k_ref[pl.ds(h, kv_rows, stride=heads), :].astype(BF16)
            vh = v_ref[pl.ds(h, kv_rows, stride=heads), :].astype(BF16)
        else:
            kh = k_ref[:, sl].astype(BF16)
            vh = v_ref[:, sl].astype(BF16)
        s = _dot_nt(q[:, sl].astype(BF16), kh) * scale
        if nb > 1:
            s = jnp.where(own, s, -1e30)
        p = jnp.exp(s - jnp.max(s, axis=-1, keepdims=True))
        l = jnp.sum(p, axis=-1, keepdims=True)
        outs.append((_dot(p.astype(BF16), vh) / l).astype(BF16))
    o_ref[...] = x + _dot(jnp.concatenate(outs, axis=1), wo_ref[...])


def _xattn(x, g, wq, k2d, v2d, wo, *, nb, lq, mem, heads):
    m, d = x.shape
    xa = wq.shape[1]
    interleaved = k2d.shape[1] != xa
    kv_blk = (nb * mem * heads, xa // heads) if interleaved else (nb * mem, xa)
    rows = nb * lq
    assert m % rows == 0
    steps = m // rows
    nseq = k2d.shape[0] * nb // kv_blk[0]
    tiles = steps * nb // nseq
    assert tiles * nseq == steps * nb and (nb == 1 or tiles == 1)
    kv_map = lambda i: (i // tiles, 0)
    const = lambda i: (0, 0)
    return pl.pallas_call(
        functools.partial(_xattn_kernel, nb=nb, lq=lq, heads=heads, interleaved=interleaved),
        grid=(steps,),
        in_specs=[
            pl.BlockSpec((rows, d), lambda i: (i, 0)),
            pl.BlockSpec((1, d), const),
            pl.BlockSpec((d, xa), const),
            pl.BlockSpec(kv_blk, kv_map),
            pl.BlockSpec(kv_blk, kv_map),
            pl.BlockSpec((xa, d), const),
        ],
        out_specs=pl.BlockSpec((rows, d), lambda i: (i, 0)),
        out_shape=jax.ShapeDtypeStruct((m, d), F32),
        compiler_params=_params("parallel"),
        name="xattn",
    )(x, g, wq, k2d, v2d, wo)


def kernel(x_prompt, x_sample, mem_prompt, state_hgrn, state_conv, cache_mem_k, cache_mem_v, norm_ffn1, ffn1_w_gate, ffn1_w_up, ffn1_w_down, norm_mix, w_in, hgrn_lb, hgrn_gnorm, conv_w, conv_b, conv_ln_g, conv_ln_b, w_out, norm_xattn, norm_mem, xattn_wq, xattn_wk, xattn_wv, xattn_wo, norm_ffn2, ffn2_w_gate, ffn2_w_up, ffn2_w_down, norm_final):
    depth = norm_ffn1.shape[0]
    assert depth == 1
    nb_p, seq, d = x_prompt.shape
    nb_s, dseq, _ = x_sample.shape
    mem = mem_prompt.shape[1]
    heads_a, expand, vdim = state_hgrn.shape[2:]
    c_conv = state_conv.shape[-1]
    xa_heads, xa_hd = cache_mem_k.shape[3:]
    xa = xa_heads * xa_hd
    assert expand == LANE and vdim == LANE and c_conv == heads_a * LANE
    za_col = (2 * heads_a * expand + 2 * heads_a * vdim) // c_conv

    l = 0
    row = lambda p: p.reshape(1, -1).astype(F32)
    bf = lambda w: w[l].astype(BF16)
    w_out16 = bf(w_out)
    wq16, wk16, wv16, wo16 = bf(xattn_wq), bf(xattn_wk), bf(xattn_wv), bf(xattn_wo)
    n_mix, n_xa, n_mem, n_fin = row(norm_mix[l]), row(norm_xattn[l]), row(norm_mem[l]), row(norm_final)
    lbp = hgrn_lb.astype(F32)
    gnorm = row(hgrn_gnorm[l])
    nc = c_conv // LANE
    chan3 = lambda p: p.reshape(-1, nc, LANE).transpose(1, 0, 2).astype(F32)
    w3, cb3, lg3, lb3 = chan3(conv_w[l]), chan3(conv_b[l]), chan3(conv_ln_g[l]), chan3(conv_ln_b[l])
    ones_d = jnp.ones((1, d), F32)

    def block(x, k2d, v2d, s0, buf0, f1w, w_in_x, f2w, *, hg, cv, xat, tm, tf, tn):
        x = _ffn(x, row(norm_ffn1[l]), *f1w, ones_d, final_norm=False, tf=tf)
        if f1w[0].dtype == F32:
            x, f1w = x
        z = _norm_matmul(x, n_mix, w_in_x, tm=tm, tn=tn)
        if w_in_x.dtype == F32:
            z, w_in_x = z
        o_a, s_new = _hgrn(z, lbp, gnorm, s0, **hg)
        o_b, buf_new = _conv(z, buf0, w3, cb3, lg3, lb3, za_col=za_col, **cv)
        x = _outproj(x, o_a, o_b, w_out16, tm=512, tn=d)
        x = _xattn(x, n_xa, wq16, k2d, v2d, wo16, mem=mem, heads=xa_heads, **xat)
        y = _ffn(x, row(norm_ffn2[l]), *f2w, n_fin, final_norm=True, tf=tf)
        if f2w[0].dtype == F32:
            y, f2w = y
        return y, s_new, buf_new, (f1w, w_in_x, f2w)

    ys, shs, scs, w16 = block(
        x_sample.reshape(nb_s * dseq, d),
        cache_mem_k[l].reshape(nb_s * mem * xa_heads, xa_hd),
        cache_mem_v[l].reshape(nb_s * mem * xa_heads, xa_hd),
        state_hgrn[l], state_conv[l],
        (ffn1_w_gate[l], ffn1_w_up[l], ffn1_w_down[l]), w_in[l],
        (ffn2_w_gate[l], ffn2_w_up[l], ffn2_w_down[l]),
        hg=dict(bk=dseq, tiles=1, hs=4),
        cv=dict(ns=8, ls=dseq, tiles=1),
        xat=dict(nb=8, lq=dseq), tm=1024, tf=256, tn=512)

    memx = mem_prompt.reshape(nb_p * mem, d)
    mk = _norm_matmul(memx, n_mem, wk16, tm=512, tn=xa)
    mv = _norm_matmul(memx, n_mem, wv16, tm=512, tn=xa)
    s0p = jnp.zeros((nb_p, heads_a, expand, vdim), state_hgrn.dtype)
    b0p = jnp.zeros((nb_p, HIST, c_conv), state_conv.dtype)
    yp, shp, scp, _ = block(
        x_prompt.reshape(nb_p * seq, d), mk, mv, s0p, b0p, *w16,
        hg=dict(bk=ROWS, tiles=seq // ROWS, hs=heads_a, tps=4),
        cv=dict(ns=1, ls=256, tiles=seq // 256),
        xat=dict(nb=1, lq=512), tm=1024, tf=512, tn=1024)

    return (yp.reshape(nb_p, seq, d), ys.reshape(nb_s, dseq, d),
            shp[None], scp[None],
            mk.reshape(1, nb_p, mem, xa_heads, xa_hd).astype(cache_mem_k.dtype),
            mv.reshape(1, nb_p, mem, xa_heads, xa_hd).astype(cache_mem_v.dtype),
            shs[None], scs[None])
```

```python
import functools

import numpy as np
import jax
import jax.numpy as jnp
from jax import lax
from jax.experimental import pallas as pl
from jax.experimental.pallas import tpu as pltpu

F32 = jnp.float32
BF16 = jnp.bfloat16
EPS = 1e-6
FFN_RES = 0.5
LANE = 128
ROWS = 128
CONV_W = 31
HIST = CONV_W - 1
VMEM_LIMIT = 56 * 1024 * 1024


def _params(*sem):
    return pltpu.CompilerParams(dimension_semantics=sem, vmem_limit_bytes=VMEM_LIMIT)


def _sigmoid(x):
    return 1.0 / (1.0 + jnp.exp(-x))


def _rms(x, g):
    return x * lax.rsqrt(jnp.mean(x * x, axis=-1, keepdims=True) + EPS) * g


def _dot(a, b):
    return jnp.dot(a, b, preferred_element_type=F32)


def _dot_nt(a, b):
    return lax.dot_general(a, b, (((1,), (1,)), ((), ())), preferred_element_type=F32)


def _dot_tn(a, b):
    return lax.dot_general(a, b, (((0,), (0,)), ((), ())), preferred_element_type=F32)


def _split3(x):
    hi = x.astype(BF16)
    r1 = x - hi.astype(F32)
    mid = r1.astype(BF16)
    lo = (r1 - mid.astype(F32)).astype(BF16)
    return hi, mid, lo


def _ffn_kernel(x_ref, g_ref, wg_ref, wu_ref, wd_ref, gf_ref, o_ref, *rest, final_norm, emit_bf16):
    j = pl.program_id(1)
    xn_ref = rest[-1]
    if emit_bf16:
        for src, dst in zip((wg_ref, wu_ref, wd_ref), rest[:3]):
            dst[...] = src[...].astype(BF16)
        wg_ref, wu_ref, wd_ref = rest[:3]

    slab = 128
    nslab = x_ref.shape[0] // slab

    @pl.when(j == 0)
    def _():
        def body(c, carry):
            r = pl.ds(pl.multiple_of(c * slab, slab), slab)
            x = x_ref[r, :]
            xn_ref[r, :] = _rms(x, g_ref[...]).astype(BF16)
            o_ref[r, :] = x
            return carry
        lax.fori_loop(0, nslab, body, 0)

    xn = xn_ref[...]
    g = _dot(xn, wg_ref[...])
    u = _dot(xn, wu_ref[...])
    h = (FFN_RES * (g * _sigmoid(g)) * u).astype(BF16)
    o_ref[...] += _dot(h, wd_ref[...])

    if final_norm:
        @pl.when(j == pl.num_programs(1) - 1)
        def _():
            def body(c, carry):
                r = pl.ds(pl.multiple_of(c * slab, slab), slab)
                o_ref[r, :] = _rms(o_ref[r, :], gf_ref[...])
                return carry
            lax.fori_loop(0, nslab, body, 0)


def _ffn(x, g, wg, wu, wd, gf, *, final_norm, tm=1024, tf=512):
    m, d = x.shape
    f = wg.shape[1]
    emit_bf16 = wg.dtype == F32
    if emit_bf16:
        tm = m
    assert m % tm == 0 and f % tf == 0
    row_mode = dict(pipeline_mode=pl.Buffered(1)) if m == tm else {}
    wspecs = [pl.BlockSpec((d, tf), lambda i, j: (0, j)),
              pl.BlockSpec((d, tf), lambda i, j: (0, j)),
              pl.BlockSpec((tf, d), lambda i, j: (j, 0))]
    out_specs = [pl.BlockSpec((tm, d), lambda i, j: (i, 0), **row_mode)]
    out_shape = [jax.ShapeDtypeStruct((m, d), F32)]
    if emit_bf16:
        out_specs += wspecs
        out_shape += [jax.ShapeDtypeStruct(w.shape, BF16) for w in (wg, wu, wd)]
    res = pl.pallas_call(
        functools.partial(_ffn_kernel, final_norm=final_norm, emit_bf16=emit_bf16),
        grid=(m // tm, f // tf),
        in_specs=[
            pl.BlockSpec((tm, d), lambda i, j: (i, 0), **row_mode),
            pl.BlockSpec((1, d), lambda i, j: (0, 0)),
            *wspecs,
            pl.BlockSpec((1, d), lambda i, j: (0, 0)),
        ],
        out_specs=out_specs,
        out_shape=out_shape,
        scratch_shapes=[pltpu.VMEM((tm, d), BF16)],
        compiler_params=_params("parallel", "arbitrary"),
        name="ffn_cast" if emit_bf16 else "ffn",
    )(x, g, wg, wu, wd, gf)
    return (res[0], tuple(res[1:])) if emit_bf16 else res[0]


def _norm_matmul_kernel(x_ref, g_ref, w_ref, o_ref, *rest, emit_bf16):
    xn_ref = rest[-1]
    if emit_bf16:
        rest[0][...] = w_ref[...].astype(BF16)
        w_ref = rest[0]

    @pl.when(pl.program_id(1) == 0)
    def _():
        xn_ref[...] = _rms(x_ref[...], g_ref[...]).astype(BF16)

    o_ref[...] = _dot(xn_ref[...], w_ref[...])


def _norm_matmul(x, g, w, *, tm, tn):
    m, d = x.shape
    n = w.shape[1]
    emit_bf16 = w.dtype == F32
    if emit_bf16:
        tm = m
    assert m % tm == 0 and n % tn == 0
    wspec = pl.BlockSpec((d, tn), lambda i, j: (0, j))
    out_specs = [pl.BlockSpec((tm, tn), lambda i, j: (i, j))]
    out_shape = [jax.ShapeDtypeStruct((m, n), F32)]
    if emit_bf16:
        out_specs.append(wspec)
        out_shape.append(jax.ShapeDtypeStruct(w.shape, BF16))
    res = pl.pallas_call(
        functools.partial(_norm_matmul_kernel, emit_bf16=emit_bf16),
        grid=(m // tm, n // tn),
        in_specs=[
            pl.BlockSpec((tm, d), lambda i, j: (i, 0)),
            pl.BlockSpec((1, d), lambda i, j: (0, 0)),
            wspec,
        ],
        out_specs=out_specs,
        out_shape=out_shape,
        scratch_shapes=[pltpu.VMEM((tm, d), BF16)],
        compiler_params=_params("parallel", "arbitrary"),
        name="norm_matmul_cast" if emit_bf16 else "norm_matmul",
    )(x, g, w)
    return tuple(res) if emit_bf16 else res[0]


def _outproj_kernel(x_ref, a_ref, b_ref, wa_ref, wb_ref, o_ref):
    o_ref[...] = x_ref[...] + _dot(a_ref[...], wa_ref[...]) + _dot(b_ref[...], wb_ref[...])


def _outproj(x, a, b, w, *, tm, tn):
    m, d = x.shape
    ka = a.shape[1]
    assert w.shape[0] == 2 * ka and b.shape[1] == ka and m % tm == 0 and d % tn == 0
    return pl.pallas_call(
        _outproj_kernel,
        grid=(m // tm, d // tn),
        in_specs=[
            pl.BlockSpec((tm, tn), lambda i, j: (i, j)),
            pl.BlockSpec((tm, ka), lambda i, j: (i, 0)),
            pl.BlockSpec((tm, ka), lambda i, j: (i, 0)),
            pl.BlockSpec((ka, tn), lambda i, j: (0, j)),
            pl.BlockSpec((ka, tn), lambda i, j: (1, j)),
        ],
        out_specs=pl.BlockSpec((tm, tn), lambda i, j: (i, j)),
        out_shape=jax.ShapeDtypeStruct((m, d), F32),
        compiler_params=_params("parallel", "arbitrary"),
        name="outproj",
    )(x, a, b, w, w)


def _hgrn_structure(bk):
    r = np.arange(ROWS)
    blk = r // bk
    tri = (blk[:, None] == blk[None, :]) & (r[None, :] <= r[:, None])
    masks = [np.eye(ROWS, dtype=bool)]
    m = 1
    while m < bk:
        grp = r // (2 * m)
        is_r = (r % (2 * m)) >= m
        masks.append((grp[:, None] == grp[None, :]) & is_r[:, None] & (~is_r)[None, :])
        m *= 2
    nblk = ROWS // bk
    rsel = blk[:, None] == (np.arange(nblk * LANE) // LANE)[None, :]
    return jnp.asarray(tri, BF16), jnp.asarray(np.stack(masks), F32), jnp.asarray(rsel, F32), nblk


def _hgrn_gates(zf_ref, lb_ref, tri_ref):
    lbr = lb_ref[...]
    e = jnp.exp(lbr - jnp.max(lbr, axis=0, keepdims=True))
    lb = e[0:1] / jnp.sum(e, axis=0, keepdims=True)
    f = lb + (1.0 - lb) * _sigmoid(zf_ref[...])
    parts = _split3(jnp.log2(f))
    tri = tri_ref[...]
    b = _dot(tri, parts[0]) + _dot(tri, parts[1]) + _dot(tri, parts[2])
    return f, parts, b


def _level_signs(row, bk):
    sgns = []
    m = 2
    while m < bk:
        sgns.append(jnp.where((row & (2 * m - 1)) >= m, 1.0, -1.0))
        m *= 2
    return sgns


def _level_weights(f, b, row, sgns, bk):
    width = b.shape[1]
    ws = [jnp.where((row & 1) == 1, f, 1.0)]
    m = 2
    while m < bk:
        g = max(2 * m, 8)
        bg = b.reshape(ROWS // g, g, width)
        if 2 * m >= 8:
            ref = bg[:, m - 1:m, :]
        else:
            sub = row.reshape(ROWS // 8, 8, width) & 7
            ref = jnp.where(sub < 4, bg[:, 1:2, :], bg[:, 5:6, :])
        ws.append(jnp.exp2((bg - ref).reshape(ROWS, width) * sgns[len(ws) - 1]))
        m *= 2
    return ws


def _intra_scores(q, k, ws, mask_ref):
    a = mask_ref[0] * _dot_nt(q.astype(BF16), k.astype(BF16))
    for l, w in enumerate(ws):
        a = a + mask_ref[1 + l] * _dot_nt((q * w).astype(BF16), (k * w).astype(BF16))
    return a.astype(BF16)


def _hgrn_seq_kernel(zq_ref, zf_ref, zi_ref, zg_ref, lb_ref, gn_ref, tri_ref, mask_ref, s0_ref,
                     o_ref, sout_ref, st_scr, *, heads):
    t = pl.program_id(1)

    @pl.when(t == 0)
    def _():
        for h in range(heads):
            st_scr[h] = s0_ref[0, h].T

    row = lax.broadcasted_iota(jnp.int32, (ROWS, LANE), 0)
    sgns = _level_signs(row, ROWS)
    for tt in range(zq_ref.shape[0] // ROWS):
        rs = slice(tt * ROWS, (tt + 1) * ROWS)
        f_all, _, b_all = _hgrn_gates(zf_ref.at[rs, :], lb_ref, tri_ref)
        for h in range(heads):
            sl = slice(h * LANE, (h + 1) * LANE)
            zq = zq_ref[rs, sl]
            q = zq * _sigmoid(zq)
            f = f_all[:, sl]
            k = 1.0 - f
            b = b_all[:, sl]
            a = _intra_scores(q, k, _level_weights(f, b, row, sgns, ROWS), mask_ref)
            v16 = zi_ref[rs, sl].astype(BF16)
            st = st_scr[h]
            bend = b[ROWS - 1:ROWS, :]
            o = _dot(a, v16) + _dot_nt((q * jnp.exp2(b)).astype(BF16), st.astype(BF16))
            kc = (k * jnp.exp2(bend - b)).astype(BF16)
            st_scr[h] = jnp.exp2(bend) * st + _dot_tn(v16, kc)
            zg = zg_ref[rs, sl]
            o_ref[rs, sl] = (_rms(o, gn_ref[...]) * (zg * _sigmoid(zg))).astype(o_ref.dtype)

    @pl.when(t == pl.num_programs(1) - 1)
    def _():
        for h in range(heads):
            sout_ref[0, h] = st_scr[h].T


def _hgrn_blocks_kernel(zq_ref, zf_ref, zi_ref, zg_ref, lb_ref, gn_ref, tri_ref, mask_ref, rsel_ref, s0_ref,
                        o_ref, sout_ref, *, heads, bk, nblk):
    f_all, parts, b_all = _hgrn_gates(zf_ref, lb_ref, tri_ref)
    row = lax.broadcasted_iota(jnp.int32, (ROWS, LANE), 0)
    sgns = _level_signs(row, bk)
    rsel = rsel_ref[...]
    rs16 = rsel.astype(BF16)
    rs16x3 = jnp.concatenate([rs16] * 3, axis=0)
    for h in range(heads):
        sl = slice(h * LANE, (h + 1) * LANE)
        zq = zq_ref[:, sl]
        q = zq * _sigmoid(zq)
        f = f_all[:, sl]
        k = 1.0 - f
        b = b_all[:, sl]
        a = _intra_scores(q, k, _level_weights(f, b, row, sgns, bk), mask_ref)
        v = zi_ref[:, sl]
        scat = jnp.concatenate([s0_ref[i, h] for i in range(nblk)], axis=1)
        y = _dot((q * jnp.exp2(b)).astype(BF16), scat.astype(BF16))
        o = _dot(a, v.astype(BF16)) + jnp.concatenate(
            [y[i * bk:(i + 1) * bk, i * LANE:(i + 1) * LANE] for i in range(nblk)], axis=0)
        bg = b.reshape(nblk, bk, LANE)
        c = (bg[:, bk - 1:bk, :] - bg).reshape(ROWS, LANE)
        kc = (k * jnp.exp2(c)).astype(BF16)
        vblk = (jnp.concatenate([v] * nblk, axis=1) * rsel).astype(BF16)
        bend_t = _dot_tn(jnp.concatenate([p[:, sl] for p in parts], axis=0), rs16x3)
        snew = jnp.exp2(bend_t) * scat + _dot_tn(kc, vblk)
        for i in range(nblk):
            sout_ref[i, h] = snew[:, i * LANE:(i + 1) * LANE]
        zg = zg_ref[:, sl]
        o_ref[:, sl] = (_rms(o, gn_ref[...]) * (zg * _sigmoid(zg))).astype(o_ref.dtype)


def _hgrn(z, lbp, gnorm, s0, *, bk, tiles, hs, tps=1):
    m = z.shape[0]
    nseq, heads = s0.shape[0], s0.shape[1]
    tri, masks, rsel, nblk = _hgrn_structure(bk)
    groups = m // (ROWS * tiles)
    assert groups * nblk == nseq and groups * tiles * ROWS == m and heads % hs == 0 and tiles % tps == 0
    steps = tiles // tps
    hb = heads // hs
    w = hs * LANE
    seq_mode = nblk == 1

    def zspec(part):
        if seq_mode:
            return pl.BlockSpec((tps * ROWS, w), lambda g, j: (g * steps + j, part))
        return pl.BlockSpec((ROWS, w), lambda g, j: (g, part * hb + j))

    const2 = lambda g, j: (0, 0)
    const3 = lambda g, j: (0, 0, 0)
    common = [zspec(0), zspec(1), zspec(2), zspec(3),
              pl.BlockSpec((lbp.shape[0], w), const2 if seq_mode else (lambda g, j: (0, j))),
              pl.BlockSpec((1, LANE), const2),
              pl.BlockSpec(tri.shape, const2),
              pl.BlockSpec(masks.shape, const3)]
    out_shape = [jax.ShapeDtypeStruct((m, heads * LANE), BF16), jax.ShapeDtypeStruct(s0.shape, s0.dtype)]
    if seq_mode:
        assert hs == heads
        sspec = pl.BlockSpec((1, heads, LANE, LANE), lambda g, j: (g, 0, 0, 0))
        return pl.pallas_call(
            functools.partial(_hgrn_seq_kernel, heads=heads),
            grid=(groups, steps),
            in_specs=common + [sspec],
            out_specs=[pl.BlockSpec((tps * ROWS, w), lambda g, j: (g * steps + j, 0)), sspec],
            out_shape=out_shape,
            scratch_shapes=[pltpu.VMEM((heads, LANE, LANE), F32)],
            compiler_params=_params("parallel", "arbitrary"),
            name="hgrn_seq",
        )(z, z, z, z, lbp, gnorm, tri, masks, s0)
    assert tiles == 1
    sspec = pl.BlockSpec((nblk, hs, LANE, LANE), lambda g, j: (g, j, 0, 0))
    return pl.pallas_call(
        functools.partial(_hgrn_blocks_kernel, heads=hs, bk=bk, nblk=nblk),
        grid=(groups, hb),
        in_specs=common + [pl.BlockSpec(rsel.shape, const2), sspec],
        out_specs=[pl.BlockSpec((ROWS, w), lambda g, j: (g, j)), sspec],
        out_shape=out_shape,
        compiler_params=_params("parallel", "parallel"),
        name="hgrn_blocks",
    )(z, z, z, z, lbp, gnorm, tri, masks, rsel, s0)


def _conv_kernel(za_ref, zb_ref, buf_ref, w_ref, cb_ref, lg_ref, lb_ref, o_ref, bufout_ref, ucat, dw,
                 *, ns, ls, nc):
    u = za_ref[...] * _sigmoid(zb_ref[...])

    @pl.when(pl.program_id(1) == 0)
    def _():
        for i in range(ns):
            for cb in range(nc):
                ucat[i, cb, 0:HIST, :] = buf_ref[i, :, cb * LANE:(cb + 1) * LANE]

    for i in range(ns):
        for cb in range(nc):
            ucat[i, cb, HIST:HIST + ls, :] = u[i * ls:(i + 1) * ls, cb * LANE:(cb + 1) * LANE]

    rt = min(ls, 64)

    def cb_body(cb, carry):
        for i in range(ns):
            for r0 in range(0, ls, rt):
                acc = jnp.broadcast_to(cb_ref[cb], (rt, LANE))
                for j in range(CONV_W):
                    acc = acc + ucat[i, cb, r0 + j:r0 + j + rt, :] * w_ref[cb, j:j + 1, :]
                dw[i, cb, r0:r0 + rt, :] = acc
        return carry

    lax.fori_loop(0, nc, cb_body, 0)

    rt2 = min(ls, 32)
    inv_c = 1.0 / (nc * LANE)
    for i in range(ns):
        for r0 in range(0, ls, rt2):
            x = dw[i, :, r0:r0 + rt2, :]
            mu = jnp.sum(jnp.sum(x, axis=0), axis=-1, keepdims=True) * inv_c
            xc = x - mu[None]
            var = jnp.sum(jnp.sum(xc * xc, axis=0), axis=-1, keepdims=True) * inv_c
            y = xc * lax.rsqrt(var + EPS)[None] * lg_ref[...] + lb_ref[...]
            y = y * _sigmoid(y)
            for cb in range(nc):
                o_ref[i * ls + r0:i * ls + r0 + rt2, cb * LANE:(cb + 1) * LANE] = y[cb].astype(o_ref.dtype)

    for i in range(ns):
        for cb in range(nc):
            tail = ucat[i, cb, ls:ls + HIST, :]
            bufout_ref[i, :, cb * LANE:(cb + 1) * LANE] = tail
            ucat[i, cb, 0:HIST, :] = tail


def _conv(z, buf0, w3, cb3, lg3, lb3, *, ns, ls, tiles, za_col):
    m = z.shape[0]
    nseq, _, c = buf0.shape
    nc = c // LANE
    groups = nseq // ns
    assert groups * tiles * ns * ls == m
    rows = ns * ls
    pad_rows = -(-(HIST + ls) // 8) * 8
    bspec = pl.BlockSpec((ns, HIST, c), lambda g, t: (g, 0, 0))
    const3 = lambda g, t: (0, 0, 0)
    return pl.pallas_call(
        functools.partial(_conv_kernel, ns=ns, ls=ls, nc=nc),
        grid=(groups, tiles),
        in_specs=[
            pl.BlockSpec((rows, c), lambda g, t: (g * tiles + t, za_col)),
            pl.BlockSpec((rows, c), lambda g, t: (g * tiles + t, za_col + 1)),
            bspec,
            pl.BlockSpec(w3.shape, const3),
            pl.BlockSpec(cb3.shape, const3),
            pl.BlockSpec(lg3.shape, const3),
            pl.BlockSpec(lb3.shape, const3),
        ],
        out_specs=[pl.BlockSpec((rows, c), lambda g, t: (g * tiles + t, 0)), bspec],
        out_shape=[jax.ShapeDtypeStruct((m, c), BF16), jax.ShapeDtypeStruct(buf0.shape, buf0.dtype)],
        scratch_shapes=[pltpu.VMEM((ns, nc, pad_rows, LANE), F32), pltpu.VMEM((ns, nc, ls, LANE), F32)],
        compiler_params=_params("parallel", "arbitrary"),
        name="conv",
    )(z, z, buf0, w3, cb3, lg3, lb3)


def _xattn_kernel(x_ref, g_ref, wq_ref, k_ref, v_ref, wo_ref, o_ref, *, nb, lq, heads, interleaved):
    x = x_ref[...]
    q = _dot(_rms(x, g_ref[...]).astype(BF16), wq_ref[...])
    hd = q.shape[1] // heads
    kv_rows = k_ref.shape[0] // heads if interleaved else k_ref.shape[0]
    mem = kv_rows // nb
    scale = hd ** -0.5
    if nb > 1:
        rows = lax.broadcasted_iota(jnp.int32, (nb * lq, nb * mem), 0) // lq
        cols = lax.broadcasted_iota(jnp.int32, (nb * lq, nb * mem), 1) // mem
        own = rows == cols
    outs = []
    for h in range(heads):
        sl = slice(h * hd, (h + 1) * hd)
        if interleaved:
            kh = k_ref[pl.ds(h, kv_rows, stride=heads), :].astype(BF16)
            vh = v_ref[pl.ds(h, kv_rows, stride=heads), :].astype(BF16)
        else:
            kh = k_ref[:, sl].astype(BF16)
            vh = v_ref[:, sl].astype(BF16)
        s = _dot_nt(q[:, sl].astype(BF16), kh) * scale
        if nb > 1:
            s = jnp.where(own, s, -1e30)
        p = jnp.exp(s - jnp.max(s, axis=-1, keepdims=True))
        l = jnp.sum(p, axis=-1, keepdims=True)
        outs.append((_dot(p.astype(BF16), vh) / l).astype(BF16))
    o_ref[...] = x + _dot(jnp.concatenate(outs, axis=1), wo_ref[...])


def _xattn(x, g, wq, k2d, v2d, wo, *, nb, lq, mem, heads):
    m, d = x.shape
    xa = wq.shape[1]
    interleaved = k2d.shape[1] != xa
    kv_blk = (nb * mem * heads, xa // heads) if interleaved else (nb * mem, xa)
    rows = nb * lq
    assert m % rows == 0
    steps = m // rows
    nseq = k2d.shape[0] * nb // kv_blk[0]
    tiles = steps * nb // nseq
    assert tiles * nseq == steps * nb and (nb == 1 or tiles == 1)
    kv_map = lambda i: (i // tiles, 0)
    const = lambda i: (0, 0)
    return pl.pallas_call(
        functools.partial(_xattn_kernel, nb=nb, lq=lq, heads=heads, interleaved=interleaved),
        grid=(steps,),
        in_specs=[
            pl.BlockSpec((rows, d), lambda i: (i, 0)),
            pl.BlockSpec((1, d), const),
            pl.BlockSpec((d, xa), const),
            pl.BlockSpec(kv_blk, kv_map),
            pl.BlockSpec(kv_blk, kv_map),
            pl.BlockSpec((xa, d), const),
        ],
        out_specs=pl.BlockSpec((rows, d), lambda i: (i, 0)),
        out_shape=jax.ShapeDtypeStruct((m, d), F32),
        compiler_params=_params("parallel"),
        name="xattn",
    )(x, g, wq, k2d, v2d, wo)


def kernel(x_prompt, x_sample, mem_prompt, state_hgrn, state_conv, cache_mem_k, cache_mem_v, norm_ffn1, ffn1_w_gate, ffn1_w_up, ffn1_w_down, norm_mix, w_in, hgrn_lb, hgrn_gnorm, conv_w, conv_b, conv_ln_g, conv_ln_b, w_out, norm_xattn, norm_mem, xattn_wq, xattn_wk, xattn_wv, xattn_wo, norm_ffn2, ffn2_w_gate, ffn2_w_up, ffn2_w_down, norm_final):
    depth = norm_ffn1.shape[0]
    assert depth == 1
    nb_p, seq, d = x_prompt.shape
    nb_s, dseq, _ = x_sample.shape
    mem = mem_prompt.shape[1]
    heads_a, expand, vdim = state_hgrn.shape[2:]
    c_conv = state_conv.shape[-1]
    xa_heads, xa_hd = cache_mem_k.shape[3:]
    xa = xa_heads * xa_hd
    assert expand == LANE and vdim == LANE and c_conv == heads_a * LANE
    za_col = (2 * heads_a * expand + 2 * heads_a * vdim) // c_conv

    l = 0
    row = lambda p: p.reshape(1, -1).astype(F32)
    bf = lambda w: w[l].astype(BF16)
    w_out16 = bf(w_out)
    wq16, wo16 = bf(xattn_wq), bf(xattn_wo)
    n_mix, n_xa, n_mem, n_fin = row(norm_mix[l]), row(norm_xattn[l]), row(norm_mem[l]), row(norm_final)
    lbp = hgrn_lb.astype(F32)
    gnorm = row(hgrn_gnorm[l])
    nc = c_conv // LANE
    chan3 = lambda p: p.reshape(-1, nc, LANE).transpose(1, 0, 2).astype(F32)
    w3, cb3, lg3, lb3 = chan3(conv_w[l]), chan3(conv_b[l]), chan3(conv_ln_g[l]), chan3(conv_ln_b[l])
    ones_d = jnp.ones((1, d), F32)

    def block(x, k2d, v2d, s0, buf0, f1w, w_in_x, f2w, *, hg, cv, xat, tm, tf, tn):
        x = _ffn(x, row(norm_ffn1[l]), *f1w, ones_d, final_norm=False, tf=tf)
        if f1w[0].dtype == F32:
            x, f1w = x
        z = _norm_matmul(x, n_mix, w_in_x, tm=tm, tn=tn)
        if w_in_x.dtype == F32:
            z, w_in_x = z
        o_a, s_new = _hgrn(z, lbp, gnorm, s0, **hg)
        o_b, buf_new = _conv(z, buf0, w3, cb3, lg3, lb3, za_col=za_col, **cv)
        x = _outproj(x, o_a, o_b, w_out16, tm=512, tn=d)
        x = _xattn(x, n_xa, wq16, k2d, v2d, wo16, mem=mem, heads=xa_heads, **xat)
        y = _ffn(x, row(norm_ffn2[l]), *f2w, n_fin, final_norm=True, tf=tf)
        if f2w[0].dtype == F32:
            y, f2w = y
        return y, s_new, buf_new, (f1w, w_in_x, f2w)

    ys, shs, scs, w16 = block(
        x_sample.reshape(nb_s * dseq, d),
        cache_mem_k[l].reshape(nb_s * mem * xa_heads, xa_hd),
        cache_mem_v[l].reshape(nb_s * mem * xa_heads, xa_hd),
        state_hgrn[l], state_conv[l],
        (ffn1_w_gate[l], ffn1_w_up[l], ffn1_w_down[l]), w_in[l],
        (ffn2_w_gate[l], ffn2_w_up[l], ffn2_w_down[l]),
        hg=dict(bk=dseq, tiles=1, hs=4),
        cv=dict(ns=8, ls=dseq, tiles=1),
        xat=dict(nb=8, lq=dseq), tm=1024, tf=256, tn=512)

    memx = mem_prompt.reshape(nb_p * mem, d)
    mk, _ = _norm_matmul(memx, n_mem, xattn_wk[l], tm=512, tn=xa)
    mv, _ = _norm_matmul(memx, n_mem, xattn_wv[l], tm=512, tn=xa)
    s0p = jnp.zeros((nb_p, heads_a, expand, vdim), state_hgrn.dtype)
    b0p = jnp.zeros((nb_p, HIST, c_conv), state_conv.dtype)
    yp, shp, scp, _ = block(
        x_prompt.reshape(nb_p * seq, d), mk, mv, s0p, b0p, *w16,
        hg=dict(bk=ROWS, tiles=seq // ROWS, hs=heads_a, tps=4),
        cv=dict(ns=1, ls=256, tiles=seq // 256),
        xat=dict(nb=1, lq=512), tm=1024, tf=512, tn=1536)

    return (yp.reshape(nb_p, seq, d), ys.reshape(nb_s, dseq, d),
            shp[None], scp[None],
            mk.reshape(1, nb_p, mem, xa_heads, xa_hd).astype(cache_mem_k.dtype),
            mv.reshape(1, nb_p, mem, xa_heads, xa_hd).astype(cache_mem_v.dtype),
            shs[None], scs[None])
```

```python
import functools

import numpy as np
import jax
import jax.numpy as jnp
from jax import lax
from jax.experimental import pallas as pl
from jax.experimental.pallas import tpu as pltpu

F32 = jnp.float32
BF16 = jnp.bfloat16
EPS = 1e-6
FFN_RES = 0.5
LANE = 128
ROWS = 128
CONV_W = 31
HIST = CONV_W - 1
VMEM_LIMIT = 56 * 1024 * 1024


def _params(*sem):
    return pltpu.CompilerParams(dimension_semantics=sem, vmem_limit_bytes=VMEM_LIMIT)


def _sigmoid(x):
    return 1.0 / (1.0 + jnp.exp(-x))


def _rms(x, g):
    return x * lax.rsqrt(jnp.mean(x * x, axis=-1, keepdims=True) + EPS) * g


def _dot(a, b):
    return jnp.dot(a, b, preferred_element_type=F32)


def _dot_nt(a, b):
    return lax.dot_general(a, b, (((1,), (1,)), ((), ())), preferred_element_type=F32)


def _dot_tn(a, b):
    return lax.dot_general(a, b, (((0,), (0,)), ((), ())), preferred_element_type=F32)


def _split3(x):
    hi = x.astype(BF16)
    r1 = x - hi.astype(F32)
    mid = r1.astype(BF16)
    lo = (r1 - mid.astype(F32)).astype(BF16)
    return hi, mid, lo


def _ffn_kernel(x_ref, g_ref, wg_ref, wu_ref, wd_ref, gf_ref, o_ref, *rest, final_norm, emit_bf16):
    j = pl.program_id(1)
    xn_ref = rest[-1]
    if emit_bf16:
        for src, dst in zip((wg_ref, wu_ref, wd_ref), rest[:3]):
            dst[...] = src[...].astype(BF16)
        wg_ref, wu_ref, wd_ref = rest[:3]

    slab = 128
    nslab = x_ref.shape[0] // slab

    @pl.when(j == 0)
    def _():
        def body(c, carry):
            r = pl.ds(pl.multiple_of(c * slab, slab), slab)
            x = x_ref[r, :]
            xn_ref[r, :] = _rms(x, g_ref[...]).astype(BF16)
            o_ref[r, :] = x
            return carry
        lax.fori_loop(0, nslab, body, 0)

    xn = xn_ref[...]
    g = _dot(xn, wg_ref[...])
    u = _dot(xn, wu_ref[...])
    h = (FFN_RES * (g * _sigmoid(g)) * u).astype(BF16)
    o_ref[...] += _dot(h, wd_ref[...])

    if final_norm:
        @pl.when(j == pl.num_programs(1) - 1)
        def _():
            def body(c, carry):
                r = pl.ds(pl.multiple_of(c * slab, slab), slab)
                o_ref[r, :] = _rms(o_ref[r, :], gf_ref[...])
                return carry
            lax.fori_loop(0, nslab, body, 0)


def _ffn(x, g, wg, wu, wd, gf, *, final_norm, tm=1024, tf=512):
    m, d = x.shape
    f = wg.shape[1]
    emit_bf16 = wg.dtype == F32
    if emit_bf16:
        tm = m
    assert m % tm == 0 and f % tf == 0
    row_mode = dict(pipeline_mode=pl.Buffered(1)) if m == tm else {}
    wspecs = [pl.BlockSpec((d, tf), lambda i, j: (0, j)),
              pl.BlockSpec((d, tf), lambda i, j: (0, j)),
              pl.BlockSpec((tf, d), lambda i, j: (j, 0))]
    out_specs = [pl.BlockSpec((tm, d), lambda i, j: (i, 0), **row_mode)]
    out_shape = [jax.ShapeDtypeStruct((m, d), F32)]
    if emit_bf16:
        out_specs += wspecs
        out_shape += [jax.ShapeDtypeStruct(w.shape, BF16) for w in (wg, wu, wd)]
    res = pl.pallas_call(
        functools.partial(_ffn_kernel, final_norm=final_norm, emit_bf16=emit_bf16),
        grid=(m // tm, f // tf),
        in_specs=[
            pl.BlockSpec((tm, d), lambda i, j: (i, 0), **row_mode),
            pl.BlockSpec((1, d), lambda i, j: (0, 0)),
            *wspecs,
            pl.BlockSpec((1, d), lambda i, j: (0, 0)),
        ],
        out_specs=out_specs,
        out_shape=out_shape,
        scratch_shapes=[pltpu.VMEM((tm, d), BF16)],
        compiler_params=_params("parallel", "arbitrary"),
        name="ffn_cast" if emit_bf16 else "ffn",
    )(x, g, wg, wu, wd, gf)
    return (res[0], tuple(res[1:])) if emit_bf16 else res[0]


def _norm_matmul_kernel(x_ref, g_ref, w_ref, o_ref, *rest, emit_bf16):
    xn_ref = rest[-1]
    if emit_bf16:
        rest[0][...] = w_ref[...].astype(BF16)
        w_ref = rest[0]

    @pl.when(pl.program_id(1) == 0)
    def _():
        xn_ref[...] = _rms(x_ref[...], g_ref[...]).astype(BF16)

    o_ref[...] = _dot(xn_ref[...], w_ref[...])


def _norm_matmul(x, g, w, *, tm, tn):
    m, d = x.shape
    n = w.shape[1]
    emit_bf16 = w.dtype == F32
    if emit_bf16:
        tm = m
    assert m % tm == 0 and n % tn == 0
    wspec = pl.BlockSpec((d, tn), lambda i, j: (0, j))
    out_specs = [pl.BlockSpec((tm, tn), lambda i, j: (i, j))]
    out_shape = [jax.ShapeDtypeStruct((m, n), F32)]
    if emit_bf16:
        out_specs.append(wspec)
        out_shape.append(jax.ShapeDtypeStruct(w.shape, BF16))
    res = pl.pallas_call(
        functools.partial(_norm_matmul_kernel, emit_bf16=emit_bf16),
        grid=(m // tm, n // tn),
        in_specs=[
            pl.BlockSpec((tm, d), lambda i, j: (i, 0)),
            pl.BlockSpec((1, d), lambda i, j: (0, 0)),
            wspec,
        ],
        out_specs=out_specs,
        out_shape=out_shape,
        scratch_shapes=[pltpu.VMEM((tm, d), BF16)],
        compiler_params=_params("parallel", "arbitrary"),
        name="norm_matmul_cast" if emit_bf16 else "norm_matmul",
    )(x, g, w)
    return tuple(res) if emit_bf16 else res[0]


def _mem_kv_kernel(x_ref, g_ref, wk_ref, wv_ref, k_ref, v_ref):
    xn = _rms(x_ref[...], g_ref[...]).astype(BF16)
    k_ref[...] = _dot(xn, wk_ref[...].astype(BF16))
    v_ref[...] = _dot(xn, wv_ref[...].astype(BF16))


def _mem_kv(x, g, wk, wv, *, tm):
    m, d = x.shape
    n = wk.shape[1]
    assert m % tm == 0 and wv.shape == wk.shape
    const = lambda i: (0, 0)
    ospec = pl.BlockSpec((tm, n), lambda i: (i, 0))
    return pl.pallas_call(
        _mem_kv_kernel,
        grid=(m // tm,),
        in_specs=[pl.BlockSpec((tm, d), lambda i: (i, 0)), pl.BlockSpec((1, d), const),
                  pl.BlockSpec((d, n), const), pl.BlockSpec((d, n), const)],
        out_specs=[ospec, ospec],
        out_shape=[jax.ShapeDtypeStruct((m, n), F32)] * 2,
        compiler_params=_params("parallel"),
        name="mem_kv",
    )(x, g, wk, wv)


def _outproj_kernel(x_ref, a_ref, b_ref, wa_ref, wb_ref, o_ref):
    o_ref[...] = x_ref[...] + _dot(a_ref[...], wa_ref[...]) + _dot(b_ref[...], wb_ref[...])


def _outproj(x, a, b, w, *, tm, tn):
    m, d = x.shape
    ka = a.shape[1]
    assert w.shape[0] == 2 * ka and b.shape[1] == ka and m % tm == 0 and d % tn == 0
    return pl.pallas_call(
        _outproj_kernel,
        grid=(m // tm, d // tn),
        in_specs=[
            pl.BlockSpec((tm, tn), lambda i, j: (i, j)),
            pl.BlockSpec((tm, ka), lambda i, j: (i, 0)),
            pl.BlockSpec((tm, ka), lambda i, j: (i, 0)),
            pl.BlockSpec((ka, tn), lambda i, j: (0, j)),
            pl.BlockSpec((ka, tn), lambda i, j: (1, j)),
        ],
        out_specs=pl.BlockSpec((tm, tn), lambda i, j: (i, j)),
        out_shape=jax.ShapeDtypeStruct((m, d), F32),
        compiler_params=_params("parallel", "arbitrary"),
        name="outproj",
    )(x, a, b, w, w)


def _hgrn_structure(bk):
    r = np.arange(ROWS)
    blk = r // bk
    tri = (blk[:, None] == blk[None, :]) & (r[None, :] <= r[:, None])
    masks = [np.eye(ROWS, dtype=bool)]
    m = 1
    while m < bk:
        grp = r // (2 * m)
        is_r = (r % (2 * m)) >= m
        masks.append((grp[:, None] == grp[None, :]) & is_r[:, None] & (~is_r)[None, :])
        m *= 2
    nblk = ROWS // bk
    rsel = blk[:, None] == (np.arange(nblk * LANE) // LANE)[None, :]
    return jnp.asarray(tri, BF16), jnp.asarray(np.stack(masks), F32), jnp.asarray(rsel, F32), nblk


def _hgrn_gates(zf_ref, lb_ref, tri_ref):
    lbr = lb_ref[...]
    e = jnp.exp(lbr - jnp.max(lbr, axis=0, keepdims=True))
    lb = e[0:1] / jnp.sum(e, axis=0, keepdims=True)
    f = lb + (1.0 - lb) * _sigmoid(zf_ref[...])
    parts = _split3(jnp.log2(f))
    tri = tri_ref[...]
    b = _dot(tri, parts[0]) + _dot(tri, parts[1]) + _dot(tri, parts[2])
    return f, parts, b


def _level_signs(row, bk):
    sgns = []
    m = 2
    while m < bk:
        sgns.append(jnp.where((row & (2 * m - 1)) >= m, 1.0, -1.0))
        m *= 2
    return sgns


def _level_weights(f, b, row, sgns, bk):
    width = b.shape[1]
    ws = [jnp.where((row & 1) == 1, f, 1.0)]
    m = 2
    while m < bk:
        g = max(2 * m, 8)
        bg = b.reshape(ROWS // g, g, width)
        if 2 * m >= 8:
            ref = bg[:, m - 1:m, :]
        else:
            sub = row.reshape(ROWS // 8, 8, width) & 7
            ref = jnp.where(sub < 4, bg[:, 1:2, :], bg[:, 5:6, :])
        ws.append(jnp.exp2((bg - ref).reshape(ROWS, width) * sgns[len(ws) - 1]))
        m *= 2
    return ws


def _intra_scores(q, k, ws, mask_ref):
    a = mask_ref[0] * _dot_nt(q.astype(BF16), k.astype(BF16))
    for l, w in enumerate(ws):
        a = a + mask_ref[1 + l] * _dot_nt((q * w).astype(BF16), (k * w).astype(BF16))
    return a.astype(BF16)


def _hgrn_seq_kernel(zq_ref, zf_ref, zi_ref, zg_ref, lb_ref, gn_ref, tri_ref, mask_ref, s0_ref,
                     o_ref, sout_ref, st_scr, *, heads):
    t = pl.program_id(1)

    @pl.when(t == 0)
    def _():
        for h in range(heads):
            st_scr[h] = s0_ref[0, h].T

    row = lax.broadcasted_iota(jnp.int32, (ROWS, LANE), 0)
    sgns = _level_signs(row, ROWS)
    for tt in range(zq_ref.shape[0] // ROWS):
        rs = slice(tt * ROWS, (tt + 1) * ROWS)
        f_all, _, b_all = _hgrn_gates(zf_ref.at[rs, :], lb_ref, tri_ref)
        for h in range(heads):
            sl = slice(h * LANE, (h + 1) * LANE)
            zq = zq_ref[rs, sl]
            q = zq * _sigmoid(zq)
            f = f_all[:, sl]
            k = 1.0 - f
            b = b_all[:, sl]
            a = _intra_scores(q, k, _level_weights(f, b, row, sgns, ROWS), mask_ref)
            v16 = zi_ref[rs, sl].astype(BF16)
            st = st_scr[h]
            bend = b[ROWS - 1:ROWS, :]
            o = _dot(a, v16) + _dot_nt((q * jnp.exp2(b)).astype(BF16), st.astype(BF16))
            kc = (k * jnp.exp2(bend - b)).astype(BF16)
            st_scr[h] = jnp.exp2(bend) * st + _dot_tn(v16, kc)
            zg = zg_ref[rs, sl]
            o_ref[rs, sl] = (_rms(o, gn_ref[...]) * (zg * _sigmoid(zg))).astype(o_ref.dtype)

    @pl.when(t == pl.num_programs(1) - 1)
    def _():
        for h in range(heads):
            sout_ref[0, h] = st_scr[h].T


def _hgrn_blocks_kernel(zq_ref, zf_ref, zi_ref, zg_ref, lb_ref, gn_ref, tri_ref, mask_ref, rsel_ref, s0_ref,
                        o_ref, sout_ref, *, heads, bk, nblk):
    f_all, parts, b_all = _hgrn_gates(zf_ref, lb_ref, tri_ref)
    row = lax.broadcasted_iota(jnp.int32, (ROWS, LANE), 0)
    sgns = _level_signs(row, bk)
    rsel = rsel_ref[...]
    rs16 = rsel.astype(BF16)
    rs16x3 = jnp.concatenate([rs16] * 3, axis=0)
    for h in range(heads):
        sl = slice(h * LANE, (h + 1) * LANE)
        zq = zq_ref[:, sl]
        q = zq * _sigmoid(zq)
        f = f_all[:, sl]
        k = 1.0 - f
        b = b_all[:, sl]
        a = _intra_scores(q, k, _level_weights(f, b, row, sgns, bk), mask_ref)
        v = zi_ref[:, sl]
        scat = jnp.concatenate([s0_ref[i, h] for i in range(nblk)], axis=1)
        y = _dot((q * jnp.exp2(b)).astype(BF16), scat.astype(BF16))
        o = _dot(a, v.astype(BF16)) + jnp.concatenate(
            [y[i * bk:(i + 1) * bk, i * LANE:(i + 1) * LANE] for i in range(nblk)], axis=0)
        bg = b.reshape(nblk, bk, LANE)
        c = (bg[:, bk - 1:bk, :] - bg).reshape(ROWS, LANE)
        kc = (k * jnp.exp2(c)).astype(BF16)
        vblk = (jnp.concatenate([v] * nblk, axis=1) * rsel).astype(BF16)
        bend_t = _dot_tn(jnp.concatenate([p[:, sl] for p in parts], axis=0), rs16x3)
        snew = jnp.exp2(bend_t) * scat + _dot_tn(kc, vblk)
        for i in range(nblk):
            sout_ref[i, h] = snew[:, i * LANE:(i + 1) * LANE]
        zg = zg_ref[:, sl]
        o_ref[:, sl] = (_rms(o, gn_ref[...]) * (zg * _sigmoid(zg))).astype(o_ref.dtype)


def _hgrn(z, lbp, gnorm, s0, *, bk, tiles, hs, tps=1):
    m = z.shape[0]
    nseq, heads = s0.shape[0], s0.shape[1]
    tri, masks, rsel, nblk = _hgrn_structure(bk)
    groups = m // (ROWS * tiles)
    assert groups * nblk == nseq and groups * tiles * ROWS == m and heads % hs == 0 and tiles % tps == 0
    steps = tiles // tps
    hb = heads // hs
    w = hs * LANE
    seq_mode = nblk == 1

    def zspec(part):
        if seq_mode:
            return pl.BlockSpec((tps * ROWS, w), lambda g, j: (g * steps + j, part))
        return pl.BlockSpec((ROWS, w), lambda g, j: (g, part * hb + j))

    const2 = lambda g, j: (0, 0)
    const3 = lambda g, j: (0, 0, 0)
    common = [zspec(0), zspec(1), zspec(2), zspec(3),
              pl.BlockSpec((lbp.shape[0], w), const2 if seq_mode else (lambda g, j: (0, j))),
              pl.BlockSpec((1, LANE), const2),
              pl.BlockSpec(tri.shape, const2),
              pl.BlockSpec(masks.shape, const3)]
    out_shape = [jax.ShapeDtypeStruct((m, heads * LANE), BF16), jax.ShapeDtypeStruct(s0.shape, s0.dtype)]
    if seq_mode:
        assert hs == heads
        sspec = pl.BlockSpec((1, heads, LANE, LANE), lambda g, j: (g, 0, 0, 0))
        return pl.pallas_call(
            functools.partial(_hgrn_seq_kernel, heads=heads),
            grid=(groups, steps),
            in_specs=common + [sspec],
            out_specs=[pl.BlockSpec((tps * ROWS, w), lambda g, j: (g * steps + j, 0)), sspec],
            out_shape=out_shape,
            scratch_shapes=[pltpu.VMEM((heads, LANE, LANE), F32)],
            compiler_params=_params("parallel", "arbitrary"),
            name="hgrn_seq",
        )(z, z, z, z, lbp, gnorm, tri, masks, s0)
    assert tiles == 1
    sspec = pl.BlockSpec((nblk, hs, LANE, LANE), lambda g, j: (g, j, 0, 0))
    return pl.pallas_call(
        functools.partial(_hgrn_blocks_kernel, heads=hs, bk=bk, nblk=nblk),
        grid=(groups, hb),
        in_specs=common + [pl.BlockSpec(rsel.shape, const2), sspec],
        out_specs=[pl.BlockSpec((ROWS, w), lambda g, j: (g, j)), sspec],
        out_shape=out_shape,
        compiler_params=_params("parallel", "parallel"),
        name="hgrn_blocks",
    )(z, z, z, z, lbp, gnorm, tri, masks, rsel, s0)


def _conv_kernel(za_ref, zb_ref, buf_ref, w_ref, cb_ref, lg_ref, lb_ref, o_ref, bufout_ref, ucat, dw,
                 *, ns, ls, nc):
    u = za_ref[...] * _sigmoid(zb_ref[...])

    @pl.when(pl.program_id(1) == 0)
    def _():
        for i in range(ns):
            for cb in range(nc):
                ucat[i, cb, 0:HIST, :] = buf_ref[i, :, cb * LANE:(cb + 1) * LANE]

    for i in range(ns):
        for cb in range(nc):
            ucat[i, cb, HIST:HIST + ls, :] = u[i * ls:(i + 1) * ls, cb * LANE:(cb + 1) * LANE]

    rt = min(ls, 64)

    def cb_body(cb, carry):
        for i in range(ns):
            for r0 in range(0, ls, rt):
                acc = jnp.broadcast_to(cb_ref[cb], (rt, LANE))
                for j in range(CONV_W):
                    acc = acc + ucat[i, cb, r0 + j:r0 + j + rt, :] * w_ref[cb, j:j + 1, :]
                dw[i, cb, r0:r0 + rt, :] = acc
        return carry

    lax.fori_loop(0, nc, cb_body, 0)

    rt2 = min(ls, 32)
    inv_c = 1.0 / (nc * LANE)
    for i in range(ns):
        for r0 in range(0, ls, rt2):
            x = dw[i, :, r0:r0 + rt2, :]
            mu = jnp.sum(jnp.sum(x, axis=0), axis=-1, keepdims=True) * inv_c
            xc = x - mu[None]
            var = jnp.sum(jnp.sum(xc * xc, axis=0), axis=-1, keepdims=True) * inv_c
            y = xc * lax.rsqrt(var + EPS)[None] * lg_ref[...] + lb_ref[...]
            y = y * _sigmoid(y)
            for cb in range(nc):
                o_ref[i * ls + r0:i * ls + r0 + rt2, cb * LANE:(cb + 1) * LANE] = y[cb].astype(o_ref.dtype)

    for i in range(ns):
        for cb in range(nc):
            tail = ucat[i, cb, ls:ls + HIST, :]
            bufout_ref[i, :, cb * LANE:(cb + 1) * LANE] = tail
            ucat[i, cb, 0:HIST, :] = tail


def _conv(z, buf0, w3, cb3, lg3, lb3, *, ns, ls, tiles, za_col):
    m = z.shape[0]
    nseq, _, c = buf0.shape
    nc = c // LANE
    groups = nseq // ns
    assert groups * tiles * ns * ls == m
    rows = ns * ls
    pad_rows = -(-(HIST + ls) // 8) * 8
    bspec = pl.BlockSpec((ns, HIST, c), lambda g, t: (g, 0, 0))
    const3 = lambda g, t: (0, 0, 0)
    return pl.pallas_call(
        functools.partial(_conv_kernel, ns=ns, ls=ls, nc=nc),
        grid=(groups, tiles),
        in_specs=[
            pl.BlockSpec((rows, c), lambda g, t: (g * tiles + t, za_col)),
            pl.BlockSpec((rows, c), lambda g, t: (g * tiles + t, za_col + 1)),
            bspec,
            pl.BlockSpec(w3.shape, const3),
            pl.BlockSpec(cb3.shape, const3),
            pl.BlockSpec(lg3.shape, const3),
            pl.BlockSpec(lb3.shape, const3),
        ],
        out_specs=[pl.BlockSpec((rows, c), lambda g, t: (g * tiles + t, 0)), bspec],
        out_shape=[jax.ShapeDtypeStruct((m, c), BF16), jax.ShapeDtypeStruct(buf0.shape, buf0.dtype)],
        scratch_shapes=[pltpu.VMEM((ns, nc, pad_rows, LANE), F32), pltpu.VMEM((ns, nc, ls, LANE), F32)],
        compiler_params=_params("parallel", "arbitrary"),
        name="conv",
    )(z, z, buf0, w3, cb3, lg3, lb3)


def _xattn_kernel(x_ref, g_ref, wq_ref, k_ref, v_ref, wo_ref, o_ref, *, nb, lq, heads, interleaved):
    x = x_ref[...]
    q = _dot(_rms(x, g_ref[...]).astype(BF16), wq_ref[...])
    hd = q.shape[1] // heads
    kv_rows = k_ref.shape[0] // heads if interleaved else k_ref.shape[0]
    mem = kv_rows // nb
    scale = hd ** -0.5
    if nb > 1:
        rows = lax.broadcasted_iota(jnp.int32, (nb * lq, nb * mem), 0) // lq
        cols = lax.broadcasted_iota(jnp.int32, (nb * lq, nb * mem), 1) // mem
        own = rows == cols
    outs = []
    for h in range(heads):
        sl = slice(h * hd, (h + 1) * hd)
        if interleaved:
            kh = k_ref[pl.ds(h, kv_rows, stride=heads), :].astype(BF16)
            vh = v_ref[pl.ds(h, kv_rows, stride=heads), :].astype(BF16)
        else:
            kh = k_ref[:, sl].astype(BF16)
            vh = v_ref[:, sl].astype(BF16)
        s = _dot_nt(q[:, sl].astype(BF16), kh) * scale
        if nb > 1:
            s = jnp.where(own, s, -1e30)
        p = jnp.exp(s - jnp.max(s, axis=-1, keepdims=True))
        l = jnp.sum(p, axis=-1, keepdims=True)
        outs.append((_dot(p.astype(BF16), vh) / l).astype(BF16))
    o_ref[...] = x + _dot(jnp.concatenate(outs, axis=1), wo_ref[...])


def _xattn(x, g, wq, k2d, v2d, wo, *, nb, lq, mem, heads):
    m, d = x.shape
    xa = wq.shape[1]
    interleaved = k2d.shape[1] != xa
    kv_blk = (nb * mem * heads, xa // heads) if interleaved else (nb * mem, xa)
    rows = nb * lq
    assert m % rows == 0
    steps = m // rows
    nseq = k2d.shape[0] * nb // kv_blk[0]
    tiles = steps * nb // nseq
    assert tiles * nseq == steps * nb and (nb == 1 or tiles == 1)
    kv_map = lambda i: (i // tiles, 0)
    const = lambda i: (0, 0)
    return pl.pallas_call(
        functools.partial(_xattn_kernel, nb=nb, lq=lq, heads=heads, interleaved=interleaved),
        grid=(steps,),
        in_specs=[
            pl.BlockSpec((rows, d), lambda i: (i, 0)),
            pl.BlockSpec((1, d), const),
            pl.BlockSpec((d, xa), const),
            pl.BlockSpec(kv_blk, kv_map),
            pl.BlockSpec(kv_blk, kv_map),
            pl.BlockSpec((xa, d), const),
        ],
        out_specs=pl.BlockSpec((rows, d), lambda i: (i, 0)),
        out_shape=jax.ShapeDtypeStruct((m, d), F32),
        compiler_params=_params("parallel"),
        name="xattn",
    )(x, g, wq, k2d, v2d, wo)


def kernel(x_prompt, x_sample, mem_prompt, state_hgrn, state_conv, cache_mem_k, cache_mem_v, norm_ffn1, ffn1_w_gate, ffn1_w_up, ffn1_w_down, norm_mix, w_in, hgrn_lb, hgrn_gnorm, conv_w, conv_b, conv_ln_g, conv_ln_b, w_out, norm_xattn, norm_mem, xattn_wq, xattn_wk, xattn_wv, xattn_wo, norm_ffn2, ffn2_w_gate, ffn2_w_up, ffn2_w_down, norm_final):
    depth = norm_ffn1.shape[0]
    assert depth == 1
    nb_p, seq, d = x_prompt.shape
    nb_s, dseq, _ = x_sample.shape
    mem = mem_prompt.shape[1]
    heads_a, expand, vdim = state_hgrn.shape[2:]
    c_conv = state_conv.shape[-1]
    xa_heads, xa_hd = cache_mem_k.shape[3:]
    xa = xa_heads * xa_hd
    assert expand == LANE and vdim == LANE and c_conv == heads_a * LANE
    za_col = (2 * heads_a * expand + 2 * heads_a * vdim) // c_conv

    l = 0
    row = lambda p: p.reshape(1, -1).astype(F32)
    bf = lambda w: w[l].astype(BF16)
    w_out16 = bf(w_out)
    wq16, wo16 = bf(xattn_wq), bf(xattn_wo)
    n_mix, n_xa, n_mem, n_fin = row(norm_mix[l]), row(norm_xattn[l]), row(norm_mem[l]), row(norm_final)
    lbp = hgrn_lb.astype(F32)
    gnorm = row(hgrn_gnorm[l])
    nc = c_conv // LANE
    chan3 = lambda p: p.reshape(-1, nc, LANE).transpose(1, 0, 2).astype(F32)
    w3, cb3, lg3, lb3 = chan3(conv_w[l]), chan3(conv_b[l]), chan3(conv_ln_g[l]), chan3(conv_ln_b[l])
    ones_d = jnp.ones((1, d), F32)

    def block(x, k2d, v2d, s0, buf0, f1w, w_in_x, f2w, *, hg, cv, xat, tm, tf, tn):
        x = _ffn(x, row(norm_ffn1[l]), *f1w, ones_d, final_norm=False, tf=tf)
        if f1w[0].dtype == F32:
            x, f1w = x
        z = _norm_matmul(x, n_mix, w_in_x, tm=tm, tn=tn)
        if w_in_x.dtype == F32:
            z, w_in_x = z
        o_a, s_new = _hgrn(z, lbp, gnorm, s0, **hg)
        o_b, buf_new = _conv(z, buf0, w3, cb3, lg3, lb3, za_col=za_col, **cv)
        x = _outproj(x, o_a, o_b, w_out16, tm=512, tn=d)
        x = _xattn(x, n_xa, wq16, k2d, v2d, wo16, mem=mem, heads=xa_heads, **xat)
        y = _ffn(x, row(norm_ffn2[l]), *f2w, n_fin, final_norm=True, tf=tf)
        if f2w[0].dtype == F32:
            y, f2w = y
        return y, s_new, buf_new, (f1w, w_in_x, f2w)

    ys, shs, scs, w16 = block(
        x_sample.reshape(nb_s * dseq, d),
        cache_mem_k[l].reshape(nb_s * mem * xa_heads, xa_hd),
        cache_mem_v[l].reshape(nb_s * mem * xa_heads, xa_hd),
        state_hgrn[l], state_conv[l],
        (ffn1_w_gate[l], ffn1_w_up[l], ffn1_w_down[l]), w_in[l],
        (ffn2_w_gate[l], ffn2_w_up[l], ffn2_w_down[l]),
        hg=dict(bk=dseq, tiles=1, hs=4),
        cv=dict(ns=8, ls=dseq, tiles=1),
        xat=dict(nb=8, lq=dseq), tm=1024, tf=256, tn=512)

    memx = mem_prompt.reshape(nb_p * mem, d)
    mk, mv = _mem_kv(memx, n_mem, xattn_wk[l], xattn_wv[l], tm=256)
    s0p = jnp.zeros((nb_p, heads_a, expand, vdim), state_hgrn.dtype)
    b0p = jnp.zeros((nb_p, HIST, c_conv), state_conv.dtype)
    yp, shp, scp, _ = block(
        x_prompt.reshape(nb_p * seq, d), mk, mv, s0p, b0p, *w16,
        hg=dict(bk=ROWS, tiles=seq // ROWS, hs=heads_a, tps=4),
        cv=dict(ns=1, ls=256, tiles=seq // 256),
        xat=dict(nb=1, lq=512), tm=1024, tf=512, tn=1536)

    return (yp.reshape(nb_p, seq, d), ys.reshape(nb_s, dseq, d),
            shp[None], scp[None],
            mk.reshape(1, nb_p, mem, xa_heads, xa_hd).astype(cache_mem_k.dtype),
            mv.reshape(1, nb_p, mem, xa_heads, xa_hd).astype(cache_mem_v.dtype),
            shs[None], scs[None])
```

```python
import functools

import numpy as np
import jax
import jax.numpy as jnp
from jax import lax
from jax.experimental import pallas as pl
from jax.experimental.pallas import tpu as pltpu

F32 = jnp.float32
BF16 = jnp.bfloat16
EPS = 1e-6
FFN_RES = 0.5
LANE = 128
ROWS = 128
CONV_W = 31
HIST = CONV_W - 1
VMEM_LIMIT = 56 * 1024 * 1024


def _params(*sem):
    return pltpu.CompilerParams(dimension_semantics=sem, vmem_limit_bytes=VMEM_LIMIT)


def _sigmoid(x):
    return 1.0 / (1.0 + jnp.exp(-x))


def _rms(x, g):
    return x * lax.rsqrt(jnp.mean(x * x, axis=-1, keepdims=True) + EPS) * g


def _dot(a, b):
    return jnp.dot(a, b, preferred_element_type=F32)


def _dot_nt(a, b):
    return lax.dot_general(a, b, (((1,), (1,)), ((), ())), preferred_element_type=F32)


def _dot_tn(a, b):
    return lax.dot_general(a, b, (((0,), (0,)), ((), ())), preferred_element_type=F32)


def _split3(x):
    hi = x.astype(BF16)
    r1 = x - hi.astype(F32)
    mid = r1.astype(BF16)
    lo = (r1 - mid.astype(F32)).astype(BF16)
    return hi, mid, lo


def _ffn_kernel(x_ref, g_ref, wg_ref, wu_ref, wd_ref, gf_ref, o_ref, *rest, final_norm, emit_bf16):
    j = pl.program_id(1)
    xn_ref = rest[-1]
    if emit_bf16:
        for src, dst in zip((wg_ref, wu_ref, wd_ref), rest[:3]):
            dst[...] = src[...].astype(BF16)
        wg_ref, wu_ref, wd_ref = rest[:3]

    slab = 128
    nslab = x_ref.shape[0] // slab

    @pl.when(j == 0)
    def _():
        def body(c, carry):
            r = pl.ds(pl.multiple_of(c * slab, slab), slab)
            x = x_ref[r, :]
            xn_ref[r, :] = _rms(x, g_ref[...]).astype(BF16)
            o_ref[r, :] = x
            return carry
        lax.fori_loop(0, nslab, body, 0)

    xn = xn_ref[...]
    g = _dot(xn, wg_ref[...])
    u = _dot(xn, wu_ref[...])
    h = (FFN_RES * (g * _sigmoid(g)) * u).astype(BF16)
    o_ref[...] += _dot(h, wd_ref[...])

    if final_norm:
        @pl.when(j == pl.num_programs(1) - 1)
        def _():
            def body(c, carry):
                r = pl.ds(pl.multiple_of(c * slab, slab), slab)
                o_ref[r, :] = _rms(o_ref[r, :], gf_ref[...])
                return carry
            lax.fori_loop(0, nslab, body, 0)


def _ffn(x, g, wg, wu, wd, gf, *, final_norm, tm=1024, tf=512):
    m, d = x.shape
    f = wg.shape[1]
    emit_bf16 = wg.dtype == F32
    if emit_bf16:
        tm = m
    assert m % tm == 0 and f % tf == 0
    row_mode = dict(pipeline_mode=pl.Buffered(1)) if m == tm else {}
    wspecs = [pl.BlockSpec((d, tf), lambda i, j: (0, j)),
              pl.BlockSpec((d, tf), lambda i, j: (0, j)),
              pl.BlockSpec((tf, d), lambda i, j: (j, 0))]
    out_specs = [pl.BlockSpec((tm, d), lambda i, j: (i, 0), **row_mode)]
    out_shape = [jax.ShapeDtypeStruct((m, d), F32)]
    if emit_bf16:
        out_specs += wspecs
        out_shape += [jax.ShapeDtypeStruct(w.shape, BF16) for w in (wg, wu, wd)]
    res = pl.pallas_call(
        functools.partial(_ffn_kernel, final_norm=final_norm, emit_bf16=emit_bf16),
        grid=(m // tm, f // tf),
        in_specs=[
            pl.BlockSpec((tm, d), lambda i, j: (i, 0), **row_mode),
            pl.BlockSpec((1, d), lambda i, j: (0, 0)),
            *wspecs,
            pl.BlockSpec((1, d), lambda i, j: (0, 0)),
        ],
        out_specs=out_specs,
        out_shape=out_shape,
        scratch_shapes=[pltpu.VMEM((tm, d), BF16)],
        compiler_params=_params("parallel", "arbitrary"),
        name="ffn_cast" if emit_bf16 else "ffn",
    )(x, g, wg, wu, wd, gf)
    return (res[0], tuple(res[1:])) if emit_bf16 else res[0]


def _norm_matmul_kernel(x_ref, g_ref, w_ref, o_ref, *rest, emit_bf16):
    xn_ref = rest[-1]
    if emit_bf16:
        rest[0][...] = w_ref[...].astype(BF16)
        w_ref = rest[0]

    @pl.when(pl.program_id(1) == 0)
    def _():
        xn_ref[...] = _rms(x_ref[...], g_ref[...]).astype(BF16)

    o_ref[...] = _dot(xn_ref[...], w_ref[...])


def _norm_matmul(x, g, w, *, tm, tn):
    m, d = x.shape
    n = w.shape[1]
    emit_bf16 = w.dtype == F32
    if emit_bf16:
        tm = m
    assert m % tm == 0 and n % tn == 0
    wspec = pl.BlockSpec((d, tn), lambda i, j: (0, j))
    out_specs = [pl.BlockSpec((tm, tn), lambda i, j: (i, j))]
    out_shape = [jax.ShapeDtypeStruct((m, n), F32)]
    if emit_bf16:
        out_specs.append(wspec)
        out_shape.append(jax.ShapeDtypeStruct(w.shape, BF16))
    res = pl.pallas_call(
        functools.partial(_norm_matmul_kernel, emit_bf16=emit_bf16),
        grid=(m // tm, n // tn),
        in_specs=[
            pl.BlockSpec((tm, d), lambda i, j: (i, 0)),
            pl.BlockSpec((1, d), lambda i, j: (0, 0)),
            wspec,
        ],
        out_specs=out_specs,
        out_shape=out_shape,
        scratch_shapes=[pltpu.VMEM((tm, d), BF16)],
        compiler_params=_params("parallel", "arbitrary"),
        name="norm_matmul_cast" if emit_bf16 else "norm_matmul",
    )(x, g, w)
    return tuple(res) if emit_bf16 else res[0]


def _mem_kv_kernel(x_ref, g_ref, wk_ref, wv_ref, k_ref, v_ref):
    xn = _rms(x_ref[...], g_ref[...]).astype(BF16)
    k_ref[...] = _dot(xn, wk_ref[...].astype(BF16))
    v_ref[...] = _dot(xn, wv_ref[...].astype(BF16))


def _mem_kv(x, g, wk, wv, *, tm):
    m, d = x.shape
    n = wk.shape[1]
    assert m % tm == 0 and wv.shape == wk.shape
    const = lambda i: (0, 0)
    ospec = pl.BlockSpec((tm, n), lambda i: (i, 0))
    return pl.pallas_call(
        _mem_kv_kernel,
        grid=(m // tm,),
        in_specs=[pl.BlockSpec((tm, d), lambda i: (i, 0)), pl.BlockSpec((1, d), const),
                  pl.BlockSpec((d, n), const), pl.BlockSpec((d, n), const)],
        out_specs=[ospec, ospec],
        out_shape=[jax.ShapeDtypeStruct((m, n), F32)] * 2,
        compiler_params=_params("parallel"),
        name="mem_kv",
    )(x, g, wk, wv)


def _outproj_kernel(x_ref, a_ref, b_ref, wa_ref, wb_ref, o_ref):
    o_ref[...] = x_ref[...] + _dot(a_ref[...], wa_ref[...]) + _dot(b_ref[...], wb_ref[...])


def _outproj(x, a, b, w, *, tm, tn):
    m, d = x.shape
    ka = a.shape[1]
    assert w.shape[0] == 2 * ka and b.shape[1] == ka and m % tm == 0 and d % tn == 0
    return pl.pallas_call(
        _outproj_kernel,
        grid=(m // tm, d // tn),
        in_specs=[
            pl.BlockSpec((tm, tn), lambda i, j: (i, j)),
            pl.BlockSpec((tm, ka), lambda i, j: (i, 0)),
            pl.BlockSpec((tm, ka), lambda i, j: (i, 0)),
            pl.BlockSpec((ka, tn), lambda i, j: (0, j)),
            pl.BlockSpec((ka, tn), lambda i, j: (1, j)),
        ],
        out_specs=pl.BlockSpec((tm, tn), lambda i, j: (i, j)),
        out_shape=jax.ShapeDtypeStruct((m, d), F32),
        compiler_params=_params("parallel", "arbitrary"),
        name="outproj",
    )(x, a, b, w, w)


def _hgrn_structure(bk):
    r = np.arange(ROWS)
    blk = r // bk
    tri = (blk[:, None] == blk[None, :]) & (r[None, :] <= r[:, None])
    masks = [np.eye(ROWS, dtype=bool)]
    m = 1
    while m < bk:
        grp = r // (2 * m)
        is_r = (r % (2 * m)) >= m
        masks.append((grp[:, None] == grp[None, :]) & is_r[:, None] & (~is_r)[None, :])
        m *= 2
    nblk = ROWS // bk
    rsel = blk[:, None] == (np.arange(nblk * LANE) // LANE)[None, :]
    return jnp.asarray(tri, BF16), jnp.asarray(np.stack(masks), F32), jnp.asarray(rsel, F32), nblk


def _hgrn_gates(zf_ref, lb_ref, tri_ref):
    lbr = lb_ref[...]
    e = jnp.exp(lbr - jnp.max(lbr, axis=0, keepdims=True))
    lb = e[0:1] / jnp.sum(e, axis=0, keepdims=True)
    f = lb + (1.0 - lb) * _sigmoid(zf_ref[...])
    parts = _split3(jnp.log2(f))
    tri = tri_ref[...]
    b = _dot(tri, parts[0]) + _dot(tri, parts[1]) + _dot(tri, parts[2])
    return f, parts, b


def _level_signs(row, bk):
    sgns = []
    m = 2
    while m < bk:
        sgns.append(jnp.where((row & (2 * m - 1)) >= m, 1.0, -1.0))
        m *= 2
    return sgns


def _level_weights(f, b, row, sgns, bk):
    width = b.shape[1]
    ws = [jnp.where((row & 1) == 1, f, 1.0)]
    m = 2
    while m < bk:
        g = max(2 * m, 8)
        bg = b.reshape(ROWS // g, g, width)
        if 2 * m >= 8:
            ref = bg[:, m - 1:m, :]
        else:
            sub = row.reshape(ROWS // 8, 8, width) & 7
            ref = jnp.where(sub < 4, bg[:, 1:2, :], bg[:, 5:6, :])
        ws.append(jnp.exp2((bg - ref).reshape(ROWS, width) * sgns[len(ws) - 1]))
        m *= 2
    return ws


def _intra_scores(q, k, ws, mask_ref):
    a = mask_ref[0] * _dot_nt(q.astype(BF16), k.astype(BF16))
    for l, w in enumerate(ws):
        a = a + mask_ref[1 + l] * _dot_nt((q * w).astype(BF16), (k * w).astype(BF16))
    return a.astype(BF16)


def _hgrn_seq_kernel(zq_ref, zf_ref, zi_ref, zg_ref, lb_ref, gn_ref, tri_ref, mask_ref, s0_ref,
                     o_ref, sout_ref, st_scr, *, heads):
    t = pl.program_id(1)

    @pl.when(t == 0)
    def _():
        for h in range(heads):
            st_scr[h] = s0_ref[0, h].T

    row = lax.broadcasted_iota(jnp.int32, (ROWS, LANE), 0)
    sgns = _level_signs(row, ROWS)
    for tt in range(zq_ref.shape[0] // ROWS):
        rs = slice(tt * ROWS, (tt + 1) * ROWS)
        f_all, _, b_all = _hgrn_gates(zf_ref.at[rs, :], lb_ref, tri_ref)
        for h in range(heads):
            sl = slice(h * LANE, (h + 1) * LANE)
            zq = zq_ref[rs, sl]
            q = zq * _sigmoid(zq)
            f = f_all[:, sl]
            k = 1.0 - f
            b = b_all[:, sl]
            a = _intra_scores(q, k, _level_weights(f, b, row, sgns, ROWS), mask_ref)
            v16 = zi_ref[rs, sl].astype(BF16)
            st = st_scr[h]
            bend = b[ROWS - 1:ROWS, :]
            o = _dot(a, v16) + _dot_nt((q * jnp.exp2(b)).astype(BF16), st.astype(BF16))
            kc = (k * jnp.exp2(bend - b)).astype(BF16)
            st_scr[h] = jnp.exp2(bend) * st + _dot_tn(v16, kc)
            zg = zg_ref[rs, sl]
            o_ref[rs, sl] = (_rms(o, gn_ref[...]) * (zg * _sigmoid(zg))).astype(o_ref.dtype)

    @pl.when(t == pl.num_programs(1) - 1)
    def _():
        for h in range(heads):
            sout_ref[0, h] = st_scr[h].T


def _hgrn_blocks_kernel(zq_ref, zf_ref, zi_ref, zg_ref, lb_ref, gn_ref, tri_ref, mask_ref, rsel_ref, s0_ref,
                        o_ref, sout_ref, *, heads, bk, nblk):
    f_all, parts, b_all = _hgrn_gates(zf_ref, lb_ref, tri_ref)
    row = lax.broadcasted_iota(jnp.int32, (ROWS, LANE), 0)
    sgns = _level_signs(row, bk)
    rsel = rsel_ref[...]
    rs16 = rsel.astype(BF16)
    rs16x3 = jnp.concatenate([rs16] * 3, axis=0)
    for h in range(heads):
        sl = slice(h * LANE, (h + 1) * LANE)
        zq = zq_ref[:, sl]
        q = zq * _sigmoid(zq)
        f = f_all[:, sl]
        k = 1.0 - f
        b = b_all[:, sl]
        a = _intra_scores(q, k, _level_weights(f, b, row, sgns, bk), mask_ref)
        v = zi_ref[:, sl]
        scat = jnp.concatenate([s0_ref[i, h] for i in range(nblk)], axis=1)
        y = _dot((q * jnp.exp2(b)).astype(BF16), scat.astype(BF16))
        o = _dot(a, v.astype(BF16)) + jnp.concatenate(
            [y[i * bk:(i + 1) * bk, i * LANE:(i + 1) * LANE] for i in range(nblk)], axis=0)
        bg = b.reshape(nblk, bk, LANE)
        c = (bg[:, bk - 1:bk, :] - bg).reshape(ROWS, LANE)
        kc = (k * jnp.exp2(c)).astype(BF16)
        vblk = (jnp.concatenate([v] * nblk, axis=1) * rsel).astype(BF16)
        bend_t = _dot_tn(jnp.concatenate([p[:, sl] for p in parts], axis=0), rs16x3)
        snew = jnp.exp2(bend_t) * scat + _dot_tn(kc, vblk)
        for i in range(nblk):
            sout_ref[i, h] = snew[:, i * LANE:(i + 1) * LANE]
        zg = zg_ref[:, sl]
        o_ref[:, sl] = (_rms(o, gn_ref[...]) * (zg * _sigmoid(zg))).astype(o_ref.dtype)


def _hgrn(z, lbp, gnorm, s0, *, bk, tiles, hs, tps=1):
    m = z.shape[0]
    nseq, heads = s0.shape[0], s0.shape[1]
    tri, masks, rsel, nblk = _hgrn_structure(bk)
    groups = m // (ROWS * tiles)
    assert groups * nblk == nseq and groups * tiles * ROWS == m and heads % hs == 0 and tiles % tps == 0
    steps = tiles // tps
    hb = heads // hs
    w = hs * LANE
    seq_mode = nblk == 1

    def zspec(part):
        if seq_mode:
            return pl.BlockSpec((tps * ROWS, w), lambda g, j: (g * steps + j, part))
        return pl.BlockSpec((ROWS, w), lambda g, j: (g, part * hb + j))

    const2 = lambda g, j: (0, 0)
    const3 = lambda g, j: (0, 0, 0)
    common = [zspec(0), zspec(1), zspec(2), zspec(3),
              pl.BlockSpec((lbp.shape[0], w), const2 if seq_mode else (lambda g, j: (0, j))),
              pl.BlockSpec((1, LANE), const2),
              pl.BlockSpec(tri.shape, const2),
              pl.BlockSpec(masks.shape, const3)]
    out_shape = [jax.ShapeDtypeStruct((m, heads * LANE), BF16), jax.ShapeDtypeStruct(s0.shape, s0.dtype)]
    if seq_mode:
        assert hs == heads
        sspec = pl.BlockSpec((1, heads, LANE, LANE), lambda g, j: (g, 0, 0, 0))
        return pl.pallas_call(
            functools.partial(_hgrn_seq_kernel, heads=heads),
            grid=(groups, steps),
            in_specs=common + [sspec],
            out_specs=[pl.BlockSpec((tps * ROWS, w), lambda g, j: (g * steps + j, 0)), sspec],
            out_shape=out_shape,
            scratch_shapes=[pltpu.VMEM((heads, LANE, LANE), F32)],
            compiler_params=_params("parallel", "arbitrary"),
            name="hgrn_seq",
        )(z, z, z, z, lbp, gnorm, tri, masks, s0)
    assert tiles == 1
    sspec = pl.BlockSpec((nblk, hs, LANE, LANE), lambda g, j: (g, j, 0, 0))
    return pl.pallas_call(
        functools.partial(_hgrn_blocks_kernel, heads=hs, bk=bk, nblk=nblk),
        grid=(groups, hb),
        in_specs=common + [pl.BlockSpec(rsel.shape, const2), sspec],
        out_specs=[pl.BlockSpec((ROWS, w), lambda g, j: (g, j)), sspec],
        out_shape=out_shape,
        compiler_params=_params("parallel", "parallel"),
        name="hgrn_blocks",
    )(z, z, z, z, lbp, gnorm, tri, masks, rsel, s0)


def _conv_kernel(za_ref, zb_ref, buf_ref, w_ref, cb_ref, lg_ref, lb_ref, o_ref, bufout_ref, ucat, dw,
                 *, ns, ls, nc):
    u = za_ref[...] * _sigmoid(zb_ref[...])

    @pl.when(pl.program_id(1) == 0)
    def _():
        for i in range(ns):
            for cb in range(nc):
                ucat[i, cb, 0:HIST, :] = buf_ref[i, :, cb * LANE:(cb + 1) * LANE]

    for i in range(ns):
        for cb in range(nc):
            ucat[i, cb, HIST:HIST + ls, :] = u[i * ls:(i + 1) * ls, cb * LANE:(cb + 1) * LANE]

    rt = min(ls, 64)

    def cb_body(cb, carry):
        for i in range(ns):
            for r0 in range(0, ls, rt):
                acc = jnp.broadcast_to(cb_ref[cb], (rt, LANE))
                for j in range(CONV_W):
                    acc = acc + ucat[i, cb, r0 + j:r0 + j + rt, :] * w_ref[cb, j:j + 1, :]
                dw[i, cb, r0:r0 + rt, :] = acc
        return carry

    lax.fori_loop(0, nc, cb_body, 0)

    rt2 = min(ls, 32)
    inv_c = 1.0 / (nc * LANE)
    for i in range(ns):
        for r0 in range(0, ls, rt2):
            x = dw[i, :, r0:r0 + rt2, :]
            mu = jnp.sum(jnp.sum(x, axis=0), axis=-1, keepdims=True) * inv_c
            xc = x - mu[None]
            var = jnp.sum(jnp.sum(xc * xc, axis=0), axis=-1, keepdims=True) * inv_c
            y = xc * lax.rsqrt(var + EPS)[None] * lg_ref[...] + lb_ref[...]
            y = y * _sigmoid(y)
            for cb in range(nc):
                o_ref[i * ls + r0:i * ls + r0 + rt2, cb * LANE:(cb + 1) * LANE] = y[cb].astype(o_ref.dtype)

    for i in range(ns):
        for cb in range(nc):
            tail = ucat[i, cb, ls:ls + HIST, :]
            bufout_ref[i, :, cb * LANE:(cb + 1) * LANE] = tail
            ucat[i, cb, 0:HIST, :] = tail


def _conv(z, buf0, w3, cb3, lg3, lb3, *, ns, ls, tiles, za_col):
    m = z.shape[0]
    nseq, _, c = buf0.shape
    nc = c // LANE
    groups = nseq // ns
    assert groups * tiles * ns * ls == m
    rows = ns * ls
    pad_rows = -(-(HIST + ls) // 8) * 8
    bspec = pl.BlockSpec((ns, HIST, c), lambda g, t: (g, 0, 0))
    const3 = lambda g, t: (0, 0, 0)
    return pl.pallas_call(
        functools.partial(_conv_kernel, ns=ns, ls=ls, nc=nc),
        grid=(groups, tiles),
        in_specs=[
            pl.BlockSpec((rows, c), lambda g, t: (g * tiles + t, za_col)),
            pl.BlockSpec((rows, c), lambda g, t: (g * tiles + t, za_col + 1)),
            bspec,
            pl.BlockSpec(w3.shape, const3),
            pl.BlockSpec(cb3.shape, const3),
            pl.BlockSpec(lg3.shape, const3),
            pl.BlockSpec(lb3.shape, const3),
        ],
        out_specs=[pl.BlockSpec((rows, c), lambda g, t: (g * tiles + t, 0)), bspec],
        out_shape=[jax.ShapeDtypeStruct((m, c), BF16), jax.ShapeDtypeStruct(buf0.shape, buf0.dtype)],
        scratch_shapes=[pltpu.VMEM((ns, nc, pad_rows, LANE), F32), pltpu.VMEM((ns, nc, ls, LANE), F32)],
        compiler_params=_params("parallel", "arbitrary"),
        name="conv",
    )(z, z, buf0, w3, cb3, lg3, lb3)


def _xattn_kernel(x_ref, g_ref, wq_ref, k_ref, v_ref, wo_ref, *rest, nb, lq, heads, interleaved):
    o_ref = rest[-1]
    x = x_ref[...]
    if len(rest) > 1:
        a_ref, b_ref, wa_ref, wb_ref = rest[:4]
        x = x + _dot(a_ref[...], wa_ref[...]) + _dot(b_ref[...], wb_ref[...])
    q = _dot(_rms(x, g_ref[...]).astype(BF16), wq_ref[...])
    hd = q.shape[1] // heads
    kv_rows = k_ref.shape[0] // heads if interleaved else k_ref.shape[0]
    mem = kv_rows // nb
    scale = hd ** -0.5
    if nb > 1:
        rows = lax.broadcasted_iota(jnp.int32, (nb * lq, nb * mem), 0) // lq
        cols = lax.broadcasted_iota(jnp.int32, (nb * lq, nb * mem), 1) // mem
        own = rows == cols
    outs = []
    for h in range(heads):
        sl = slice(h * hd, (h + 1) * hd)
        if interleaved:
            kh = k_ref[pl.ds(h, kv_rows, stride=heads), :].astype(BF16)
            vh = v_ref[pl.ds(h, kv_rows, stride=heads), :].astype(BF16)
        else:
            kh = k_ref[:, sl].astype(BF16)
            vh = v_ref[:, sl].astype(BF16)
        s = _dot_nt(q[:, sl].astype(BF16), kh) * scale
        if nb > 1:
            s = jnp.where(own, s, -1e30)
        p = jnp.exp(s - jnp.max(s, axis=-1, keepdims=True))
        l = jnp.sum(p, axis=-1, keepdims=True)
        outs.append((_dot(p.astype(BF16), vh) / l).astype(BF16))
    o_ref[...] = x + _dot(jnp.concatenate(outs, axis=1), wo_ref[...])


def _xattn(x, g, wq, k2d, v2d, wo, *, nb, lq, mem, heads, mix=None):
    m, d = x.shape
    xa = wq.shape[1]
    interleaved = k2d.shape[1] != xa
    kv_blk = (nb * mem * heads, xa // heads) if interleaved else (nb * mem, xa)
    rows = nb * lq
    assert m % rows == 0
    steps = m // rows
    nseq = k2d.shape[0] * nb // kv_blk[0]
    tiles = steps * nb // nseq
    assert tiles * nseq == steps * nb and (nb == 1 or tiles == 1)
    kv_map = lambda i: (i // tiles, 0)
    const = lambda i: (0, 0)
    in_specs = [
        pl.BlockSpec((rows, d), lambda i: (i, 0)),
        pl.BlockSpec((1, d), const),
        pl.BlockSpec((d, xa), const),
        pl.BlockSpec(kv_blk, kv_map),
        pl.BlockSpec(kv_blk, kv_map),
        pl.BlockSpec((xa, d), const),
    ]
    args = [x, g, wq, k2d, v2d, wo]
    if mix is not None:
        a, b, w = mix
        ka = a.shape[1]
        assert w.shape == (2 * ka, d) and b.shape == a.shape
        in_specs += [pl.BlockSpec((rows, ka), lambda i: (i, 0)), pl.BlockSpec((rows, ka), lambda i: (i, 0)),
                     pl.BlockSpec((ka, d), const), pl.BlockSpec((ka, d), lambda i: (1, 0))]
        args += [a, b, w, w]
    return pl.pallas_call(
        functools.partial(_xattn_kernel, nb=nb, lq=lq, heads=heads, interleaved=interleaved),
        grid=(steps,),
        in_specs=in_specs,
        out_specs=pl.BlockSpec((rows, d), lambda i: (i, 0)),
        out_shape=jax.ShapeDtypeStruct((m, d), F32),
        compiler_params=_params("parallel"),
        name="xattn" if mix is None else "outproj_xattn",
    )(*args)


def kernel(x_prompt, x_sample, mem_prompt, state_hgrn, state_conv, cache_mem_k, cache_mem_v, norm_ffn1, ffn1_w_gate, ffn1_w_up, ffn1_w_down, norm_mix, w_in, hgrn_lb, hgrn_gnorm, conv_w, conv_b, conv_ln_g, conv_ln_b, w_out, norm_xattn, norm_mem, xattn_wq, xattn_wk, xattn_wv, xattn_wo, norm_ffn2, ffn2_w_gate, ffn2_w_up, ffn2_w_down, norm_final):
    depth = norm_ffn1.shape[0]
    assert depth == 1
    nb_p, seq, d = x_prompt.shape
    nb_s, dseq, _ = x_sample.shape
    mem = mem_prompt.shape[1]
    heads_a, expand, vdim = state_hgrn.shape[2:]
    c_conv = state_conv.shape[-1]
    xa_heads, xa_hd = cache_mem_k.shape[3:]
    xa = xa_heads * xa_hd
    assert expand == LANE and vdim == LANE and c_conv == heads_a * LANE
    za_col = (2 * heads_a * expand + 2 * heads_a * vdim) // c_conv

    l = 0
    row = lambda p: p.reshape(1, -1).astype(F32)
    bf = lambda w: w[l].astype(BF16)
    w_out16 = bf(w_out)
    wq16, wo16 = bf(xattn_wq), bf(xattn_wo)
    n_mix, n_xa, n_mem, n_fin = row(norm_mix[l]), row(norm_xattn[l]), row(norm_mem[l]), row(norm_final)
    lbp = hgrn_lb.astype(F32)
    gnorm = row(hgrn_gnorm[l])
    nc = c_conv // LANE
    chan3 = lambda p: p.reshape(-1, nc, LANE).transpose(1, 0, 2).astype(F32)
    w3, cb3, lg3, lb3 = chan3(conv_w[l]), chan3(conv_b[l]), chan3(conv_ln_g[l]), chan3(conv_ln_b[l])
    ones_d = jnp.ones((1, d), F32)

    def block(x, k2d, v2d, s0, buf0, f1w, w_in_x, f2w, *, hg, cv, xat, tm, tf, tn, fuse_out):
        x = _ffn(x, row(norm_ffn1[l]), *f1w, ones_d, final_norm=False, tf=tf)
        if f1w[0].dtype == F32:
            x, f1w = x
        z = _norm_matmul(x, n_mix, w_in_x, tm=tm, tn=tn)
        if w_in_x.dtype == F32:
            z, w_in_x = z
        o_a, s_new = _hgrn(z, lbp, gnorm, s0, **hg)
        o_b, buf_new = _conv(z, buf0, w3, cb3, lg3, lb3, za_col=za_col, **cv)
        if fuse_out:
            x = _xattn(x, n_xa, wq16, k2d, v2d, wo16, mem=mem, heads=xa_heads, mix=(o_a, o_b, w_out16), **xat)
        else:
            x = _outproj(x, o_a, o_b, w_out16, tm=512, tn=d)
            x = _xattn(x, n_xa, wq16, k2d, v2d, wo16, mem=mem, heads=xa_heads, **xat)
        y = _ffn(x, row(norm_ffn2[l]), *f2w, n_fin, final_norm=True, tf=tf)
        if f2w[0].dtype == F32:
            y, f2w = y
        return y, s_new, buf_new, (f1w, w_in_x, f2w)

    ys, shs, scs, w16 = block(
        x_sample.reshape(nb_s * dseq, d),
        cache_mem_k[l].reshape(nb_s * mem * xa_heads, xa_hd),
        cache_mem_v[l].reshape(nb_s * mem * xa_heads, xa_hd),
        state_hgrn[l], state_conv[l],
        (ffn1_w_gate[l], ffn1_w_up[l], ffn1_w_down[l]), w_in[l],
        (ffn2_w_gate[l], ffn2_w_up[l], ffn2_w_down[l]),
        hg=dict(bk=dseq, tiles=1, hs=4),
        cv=dict(ns=8, ls=dseq, tiles=1),
        xat=dict(nb=8, lq=dseq), tm=1024, tf=256, tn=512, fuse_out=False)

    memx = mem_prompt.reshape(nb_p * mem, d)
    mk, mv = _mem_kv(memx, n_mem, xattn_wk[l], xattn_wv[l], tm=256)
    s0p = jnp.zeros((nb_p, heads_a, expand, vdim), state_hgrn.dtype)
    b0p = jnp.zeros((nb_p, HIST, c_conv), state_conv.dtype)
    yp, shp, scp, _ = block(
        x_prompt.reshape(nb_p * seq, d), mk, mv, s0p, b0p, *w16,
        hg=dict(bk=ROWS, tiles=seq // ROWS, hs=heads_a, tps=4),
        cv=dict(ns=1, ls=256, tiles=seq // 256),
        xat=dict(nb=1, lq=512), tm=1024, tf=512, tn=1536, fuse_out=True)

    return (yp.reshape(nb_p, seq, d), ys.reshape(nb_s, dseq, d),
            shp[None], scp[None],
            mk.reshape(1, nb_p, mem, xa_heads, xa_hd).astype(cache_mem_k.dtype),
            mv.reshape(1, nb_p, mem, xa_heads, xa_hd).astype(cache_mem_v.dtype),
            shs[None], scs[None])
```

```python
import functools

import numpy as np
import jax
import jax.numpy as jnp
from jax import lax
from jax.experimental import pallas as pl
from jax.experimental.pallas import tpu as pltpu

F32 = jnp.float32
BF16 = jnp.bfloat16
EPS = 1e-6
FFN_RES = 0.5
LANE = 128
ROWS = 128
CONV_W = 31
HIST = CONV_W - 1
VMEM_LIMIT = 56 * 1024 * 1024


def _params(*sem):
    return pltpu.CompilerParams(dimension_semantics=sem, vmem_limit_bytes=VMEM_LIMIT)


def _sigmoid(x):
    return 1.0 / (1.0 + jnp.exp(-x))


def _rms(x, g):
    return x * lax.rsqrt(jnp.mean(x * x, axis=-1, keepdims=True) + EPS) * g


def _dot(a, b):
    return jnp.dot(a, b, preferred_element_type=F32)


def _dot_nt(a, b):
    return lax.dot_general(a, b, (((1,), (1,)), ((), ())), preferred_element_type=F32)


def _dot_tn(a, b):
    return lax.dot_general(a, b, (((0,), (0,)), ((), ())), preferred_element_type=F32)


def _split3(x):
    hi = x.astype(BF16)
    r1 = x - hi.astype(F32)
    mid = r1.astype(BF16)
    lo = (r1 - mid.astype(F32)).astype(BF16)
    return hi, mid, lo


def _ffn_kernel(x_ref, g_ref, wg_ref, wu_ref, wd_ref, gf_ref, o_ref, *rest, final_norm, emit_bf16):
    j = pl.program_id(1)
    xn_ref = rest[-1]
    if emit_bf16:
        for src, dst in zip((wg_ref, wu_ref, wd_ref), rest[:3]):
            dst[...] = src[...].astype(BF16)
        wg_ref, wu_ref, wd_ref = rest[:3]

    slab = 128
    nslab = x_ref.shape[0] // slab

    @pl.when(j == 0)
    def _():
        def body(c, carry):
            r = pl.ds(pl.multiple_of(c * slab, slab), slab)
            x = x_ref[r, :]
            xn_ref[r, :] = _rms(x, g_ref[...]).astype(BF16)
            o_ref[r, :] = x
            return carry
        lax.fori_loop(0, nslab, body, 0)

    xn = xn_ref[...]
    g = _dot(xn, wg_ref[...])
    u = _dot(xn, wu_ref[...])
    h = (FFN_RES * (g * _sigmoid(g)) * u).astype(BF16)
    o_ref[...] += _dot(h, wd_ref[...])

    if final_norm:
        @pl.when(j == pl.num_programs(1) - 1)
        def _():
            def body(c, carry):
                r = pl.ds(pl.multiple_of(c * slab, slab), slab)
                o_ref[r, :] = _rms(o_ref[r, :], gf_ref[...])
                return carry
            lax.fori_loop(0, nslab, body, 0)


def _ffn(x, g, wg, wu, wd, gf, *, final_norm, tm=1024, tf=512):
    m, d = x.shape
    f = wg.shape[1]
    emit_bf16 = wg.dtype == F32
    if emit_bf16:
        tm = m
    assert m % tm == 0 and f % tf == 0
    row_mode = dict(pipeline_mode=pl.Buffered(1)) if m == tm else {}
    wspecs = [pl.BlockSpec((d, tf), lambda i, j: (0, j)),
              pl.BlockSpec((d, tf), lambda i, j: (0, j)),
              pl.BlockSpec((tf, d), lambda i, j: (j, 0))]
    out_specs = [pl.BlockSpec((tm, d), lambda i, j: (i, 0), **row_mode)]
    out_shape = [jax.ShapeDtypeStruct((m, d), F32)]
    if emit_bf16:
        out_specs += wspecs
        out_shape += [jax.ShapeDtypeStruct(w.shape, BF16) for w in (wg, wu, wd)]
    res = pl.pallas_call(
        functools.partial(_ffn_kernel, final_norm=final_norm, emit_bf16=emit_bf16),
        grid=(m // tm, f // tf),
        in_specs=[
            pl.BlockSpec((tm, d), lambda i, j: (i, 0), **row_mode),
            pl.BlockSpec((1, d), lambda i, j: (0, 0)),
            *wspecs,
            pl.BlockSpec((1, d), lambda i, j: (0, 0)),
        ],
        out_specs=out_specs,
        out_shape=out_shape,
        scratch_shapes=[pltpu.VMEM((tm, d), BF16)],
        compiler_params=_params("parallel", "arbitrary"),
        name="ffn_cast" if emit_bf16 else "ffn",
    )(x, g, wg, wu, wd, gf)
    return (res[0], tuple(res[1:])) if emit_bf16 else res[0]


def _norm_matmul_kernel(x_ref, g_ref, w_ref, o_ref, *rest, emit_bf16, glu_tail):
    xn_ref = rest[-1]
    if emit_bf16:
        rest[0][...] = w_ref[...].astype(BF16)
        w_ref = rest[0]
    j = pl.program_id(1)

    @pl.when(j == 0)
    def _():
        xn_ref[...] = _rms(x_ref[...], g_ref[...]).astype(BF16)

    r = _dot(xn_ref[...], w_ref[...])
    if not glu_tail:
        o_ref[...] = r
    else:
        a_scr = rest[-2]
        last = pl.num_programs(1) - 1

        @pl.when(j < last - 1)
        def _():
            o_ref[...] = r

        @pl.when(j == last - 1)
        def _():
            a_scr[...] = r

        @pl.when(j == last)
        def _():
            o_ref[...] = a_scr[...] * _sigmoid(r)


def _norm_matmul(x, g, w, *, tm, tn, glu_tail=False):
    m, d = x.shape
    n = w.shape[1]
    emit_bf16 = w.dtype == F32
    if emit_bf16:
        tm = m
    assert m % tm == 0 and n % tn == 0 and not (glu_tail and emit_bf16)
    nj = n // tn
    wspec = pl.BlockSpec((d, tn), lambda i, j: (0, j))
    if glu_tail:
        out_specs = [pl.BlockSpec((tm, tn), lambda i, j: (i, jnp.minimum(j, nj - 2)))]
        out_shape = [jax.ShapeDtypeStruct((m, n - tn), F32)]
        scratch = [pltpu.VMEM((tm, tn), F32), pltpu.VMEM((tm, d), BF16)]
    else:
        out_specs = [pl.BlockSpec((tm, tn), lambda i, j: (i, j))]
        out_shape = [jax.ShapeDtypeStruct((m, n), F32)]
        scratch = [pltpu.VMEM((tm, d), BF16)]
    if emit_bf16:
        out_specs.append(wspec)
        out_shape.append(jax.ShapeDtypeStruct(w.shape, BF16))
    res = pl.pallas_call(
        functools.partial(_norm_matmul_kernel, emit_bf16=emit_bf16, glu_tail=glu_tail),
        grid=(m // tm, nj),
        in_specs=[
            pl.BlockSpec((tm, d), lambda i, j: (i, 0)),
            pl.BlockSpec((1, d), lambda i, j: (0, 0)),
            wspec,
        ],
        out_specs=out_specs,
        out_shape=out_shape,
        scratch_shapes=scratch,
        compiler_params=_params("parallel", "arbitrary"),
        name="norm_matmul_cast" if emit_bf16 else "norm_matmul",
    )(x, g, w)
    return tuple(res) if emit_bf16 else res[0]


def _mem_kv_kernel(x_ref, g_ref, wk_ref, wv_ref, k_ref, v_ref):
    xn = _rms(x_ref[...], g_ref[...]).astype(BF16)
    k_ref[...] = _dot(xn, wk_ref[...].astype(BF16))
    v_ref[...] = _dot(xn, wv_ref[...].astype(BF16))


def _mem_kv(x, g, wk, wv, *, tm):
    m, d = x.shape
    n = wk.shape[1]
    assert m % tm == 0 and wv.shape == wk.shape
    const = lambda i: (0, 0)
    ospec = pl.BlockSpec((tm, n), lambda i: (i, 0))
    return pl.pallas_call(
        _mem_kv_kernel,
        grid=(m // tm,),
        in_specs=[pl.BlockSpec((tm, d), lambda i: (i, 0)), pl.BlockSpec((1, d), const),
                  pl.BlockSpec((d, n), const), pl.BlockSpec((d, n), const)],
        out_specs=[ospec, ospec],
        out_shape=[jax.ShapeDtypeStruct((m, n), F32)] * 2,
        compiler_params=_params("parallel"),
        name="mem_kv",
    )(x, g, wk, wv)


def _outproj_kernel(x_ref, a_ref, b_ref, wa_ref, wb_ref, o_ref):
    o_ref[...] = x_ref[...] + _dot(a_ref[...], wa_ref[...]) + _dot(b_ref[...], wb_ref[...])


def _outproj(x, a, b, w, *, tm, tn):
    m, d = x.shape
    ka = a.shape[1]
    assert w.shape[0] == 2 * ka and b.shape[1] == ka and m % tm == 0 and d % tn == 0
    return pl.pallas_call(
        _outproj_kernel,
        grid=(m // tm, d // tn),
        in_specs=[
            pl.BlockSpec((tm, tn), lambda i, j: (i, j)),
            pl.BlockSpec((tm, ka), lambda i, j: (i, 0)),
            pl.BlockSpec((tm, ka), lambda i, j: (i, 0)),
            pl.BlockSpec((ka, tn), lambda i, j: (0, j)),
            pl.BlockSpec((ka, tn), lambda i, j: (1, j)),
        ],
        out_specs=pl.BlockSpec((tm, tn), lambda i, j: (i, j)),
        out_shape=jax.ShapeDtypeStruct((m, d), F32),
        compiler_params=_params("parallel", "arbitrary"),
        name="outproj",
    )(x, a, b, w, w)


def _hgrn_structure(bk):
    r = np.arange(ROWS)
    blk = r // bk
    tri = (blk[:, None] == blk[None, :]) & (r[None, :] <= r[:, None])
    masks = [np.eye(ROWS, dtype=bool)]
    m = 1
    while m < bk:
        grp = r // (2 * m)
        is_r = (r % (2 * m)) >= m
        masks.append((grp[:, None] == grp[None, :]) & is_r[:, None] & (~is_r)[None, :])
        m *= 2
    nblk = ROWS // bk
    rsel = blk[:, None] == (np.arange(nblk * LANE) // LANE)[None, :]
    return jnp.asarray(tri, BF16), jnp.asarray(np.stack(masks), F32), jnp.asarray(rsel, F32), nblk


def _hgrn_gates(zf_ref, lb_ref, tri_ref):
    lbr = lb_ref[...]
    e = jnp.exp(lbr - jnp.max(lbr, axis=0, keepdims=True))
    lb = e[0:1] / jnp.sum(e, axis=0, keepdims=True)
    f = lb + (1.0 - lb) * _sigmoid(zf_ref[...])
    parts = _split3(jnp.log2(f))
    tri = tri_ref[...]
    b = _dot(tri, parts[0]) + _dot(tri, parts[1]) + _dot(tri, parts[2])
    return f, parts, b


def _level_signs(row, bk):
    sgns = []
    m = 2
    while m < bk:
        sgns.append(jnp.where((row & (2 * m - 1)) >= m, 1.0, -1.0))
        m *= 2
    return sgns


def _level_weights(f, b, row, sgns, bk):
    width = b.shape[1]
    ws = [jnp.where((row & 1) == 1, f, 1.0)]
    m = 2
    while m < bk:
        g = max(2 * m, 8)
        bg = b.reshape(ROWS // g, g, width)
        if 2 * m >= 8:
            ref = bg[:, m - 1:m, :]
        else:
            sub = row.reshape(ROWS // 8, 8, width) & 7
            ref = jnp.where(sub < 4, bg[:, 1:2, :], bg[:, 5:6, :])
        ws.append(jnp.exp2((bg - ref).reshape(ROWS, width) * sgns[len(ws) - 1]))
        m *= 2
    return ws


def _intra_scores(q, k, ws, mask_ref):
    a = mask_ref[0] * _dot_nt(q.astype(BF16), k.astype(BF16))
    for l, w in enumerate(ws):
        a = a + mask_ref[1 + l] * _dot_nt((q * w).astype(BF16), (k * w).astype(BF16))
    return a.astype(BF16)


def _hgrn_seq_kernel(zq_ref, zf_ref, zi_ref, zg_ref, lb_ref, gn_ref, tri_ref, mask_ref, s0_ref,
                     o_ref, sout_ref, st_scr, *, heads):
    t = pl.program_id(1)

    @pl.when(t == 0)
    def _():
        for h in range(heads):
            st_scr[h] = s0_ref[0, h].T

    row = lax.broadcasted_iota(jnp.int32, (ROWS, LANE), 0)
    sgns = _level_signs(row, ROWS)
    for tt in range(zq_ref.shape[0] // ROWS):
        rs = slice(tt * ROWS, (tt + 1) * ROWS)
        f_all, _, b_all = _hgrn_gates(zf_ref.at[rs, :], lb_ref, tri_ref)
        for h in range(heads):
            sl = slice(h * LANE, (h + 1) * LANE)
            zq = zq_ref[rs, sl]
            q = zq * _sigmoid(zq)
            f = f_all[:, sl]
            k = 1.0 - f
            b = b_all[:, sl]
            a = _intra_scores(q, k, _level_weights(f, b, row, sgns, ROWS), mask_ref)
            v16 = zi_ref[rs, sl].astype(BF16)
            st = st_scr[h]
            bend = b[ROWS - 1:ROWS, :]
            o = _dot(a, v16) + _dot_nt((q * jnp.exp2(b)).astype(BF16), st.astype(BF16))
            kc = (k * jnp.exp2(bend - b)).astype(BF16)
            st_scr[h] = jnp.exp2(bend) * st + _dot_tn(v16, kc)
            zg = zg_ref[rs, sl]
            o_ref[rs, sl] = (_rms(o, gn_ref[...]) * (zg * _sigmoid(zg))).astype(o_ref.dtype)

    @pl.when(t == pl.num_programs(1) - 1)
    def _():
        for h in range(heads):
            sout_ref[0, h] = st_scr[h].T


def _hgrn_blocks_kernel(zq_ref, zf_ref, zi_ref, zg_ref, lb_ref, gn_ref, tri_ref, mask_ref, rsel_ref, s0_ref,
                        o_ref, sout_ref, *, heads, bk, nblk):
    f_all, parts, b_all = _hgrn_gates(zf_ref, lb_ref, tri_ref)
    row = lax.broadcasted_iota(jnp.int32, (ROWS, LANE), 0)
    sgns = _level_signs(row, bk)
    rsel = rsel_ref[...]
    rs16 = rsel.astype(BF16)
    rs16x3 = jnp.concatenate([rs16] * 3, axis=0)
    for h in range(heads):
        sl = slice(h * LANE, (h + 1) * LANE)
        zq = zq_ref[:, sl]
        q = zq * _sigmoid(zq)
        f = f_all[:, sl]
        k = 1.0 - f
        b = b_all[:, sl]
        a = _intra_scores(q, k, _level_weights(f, b, row, sgns, bk), mask_ref)
        v = zi_ref[:, sl]
        scat = jnp.concatenate([s0_ref[i, h] for i in range(nblk)], axis=1)
        y = _dot((q * jnp.exp2(b)).astype(BF16), scat.astype(BF16))
        o = _dot(a, v.astype(BF16)) + jnp.concatenate(
            [y[i * bk:(i + 1) * bk, i * LANE:(i + 1) * LANE] for i in range(nblk)], axis=0)
        bg = b.reshape(nblk, bk, LANE)
        c = (bg[:, bk - 1:bk, :] - bg).reshape(ROWS, LANE)
        kc = (k * jnp.exp2(c)).astype(BF16)
        vblk = (jnp.concatenate([v] * nblk, axis=1) * rsel).astype(BF16)
        bend_t = _dot_tn(jnp.concatenate([p[:, sl] for p in parts], axis=0), rs16x3)
        snew = jnp.exp2(bend_t) * scat + _dot_tn(kc, vblk)
        for i in range(nblk):
            sout_ref[i, h] = snew[:, i * LANE:(i + 1) * LANE]
        zg = zg_ref[:, sl]
        o_ref[:, sl] = (_rms(o, gn_ref[...]) * (zg * _sigmoid(zg))).astype(o_ref.dtype)


def _hgrn(z, lbp, gnorm, s0, *, bk, tiles, hs, tps=1):
    m = z.shape[0]
    nseq, heads = s0.shape[0], s0.shape[1]
    tri, masks, rsel, nblk = _hgrn_structure(bk)
    groups = m // (ROWS * tiles)
    assert groups * nblk == nseq and groups * tiles * ROWS == m and heads % hs == 0 and tiles % tps == 0
    steps = tiles // tps
    hb = heads // hs
    w = hs * LANE
    seq_mode = nblk == 1

    def zspec(part):
        if seq_mode:
            return pl.BlockSpec((tps * ROWS, w), lambda g, j: (g * steps + j, part))
        return pl.BlockSpec((ROWS, w), lambda g, j: (g, part * hb + j))

    const2 = lambda g, j: (0, 0)
    const3 = lambda g, j: (0, 0, 0)
    common = [zspec(0), zspec(1), zspec(2), zspec(3),
              pl.BlockSpec((lbp.shape[0], w), const2 if seq_mode else (lambda g, j: (0, j))),
              pl.BlockSpec((1, LANE), const2),
              pl.BlockSpec(tri.shape, const2),
              pl.BlockSpec(masks.shape, const3)]
    out_shape = [jax.ShapeDtypeStruct((m, heads * LANE), BF16), jax.ShapeDtypeStruct(s0.shape, s0.dtype)]
    if seq_mode:
        assert hs == heads
        sspec = pl.BlockSpec((1, heads, LANE, LANE), lambda g, j: (g, 0, 0, 0))
        return pl.pallas_call(
            functools.partial(_hgrn_seq_kernel, heads=heads),
            grid=(groups, steps),
            in_specs=common + [sspec],
            out_specs=[pl.BlockSpec((tps * ROWS, w), lambda g, j: (g * steps + j, 0)), sspec],
            out_shape=out_shape,
            scratch_shapes=[pltpu.VMEM((heads, LANE, LANE), F32)],
            compiler_params=_params("parallel", "arbitrary"),
            name="hgrn_seq",
        )(z, z, z, z, lbp, gnorm, tri, masks, s0)
    assert tiles == 1
    sspec = pl.BlockSpec((nblk, hs, LANE, LANE), lambda g, j: (g, j, 0, 0))
    return pl.pallas_call(
        functools.partial(_hgrn_blocks_kernel, heads=hs, bk=bk, nblk=nblk),
        grid=(groups, hb),
        in_specs=common + [pl.BlockSpec(rsel.shape, const2), sspec],
        out_specs=[pl.BlockSpec((ROWS, w), lambda g, j: (g, j)), sspec],
        out_shape=out_shape,
        compiler_params=_params("parallel", "parallel"),
        name="hgrn_blocks",
    )(z, z, z, z, lbp, gnorm, tri, masks, rsel, s0)


def _conv_kernel(za_ref, *rest, ns, ls, nc, glu_done):
    if glu_done:
        u = za_ref[...]
    else:
        u = za_ref[...] * _sigmoid(rest[0][...])
        rest = rest[1:]
    buf_ref, w_ref, cb_ref, lg_ref, lb_ref, o_ref, bufout_ref, ucat, dw = rest

    @pl.when(pl.program_id(1) == 0)
    def _():
        for i in range(ns):
            for cb in range(nc):
                ucat[i, cb, 0:HIST, :] = buf_ref[i, :, cb * LANE:(cb + 1) * LANE]

    for i in range(ns):
        for cb in range(nc):
            ucat[i, cb, HIST:HIST + ls, :] = u[i * ls:(i + 1) * ls, cb * LANE:(cb + 1) * LANE]

    rt = min(ls, 64)

    def cb_body(cb, carry):
        for i in range(ns):
            for r0 in range(0, ls, rt):
                acc = jnp.broadcast_to(cb_ref[cb], (rt, LANE))
                for j in range(CONV_W):
                    acc = acc + ucat[i, cb, r0 + j:r0 + j + rt, :] * w_ref[cb, j:j + 1, :]
                dw[i, cb, r0:r0 + rt, :] = acc
        return carry

    lax.fori_loop(0, nc, cb_body, 0)

    rt2 = min(ls, 32)
    inv_c = 1.0 / (nc * LANE)
    for i in range(ns):
        for r0 in range(0, ls, rt2):
            x = dw[i, :, r0:r0 + rt2, :]
            mu = jnp.sum(jnp.sum(x, axis=0), axis=-1, keepdims=True) * inv_c
            xc = x - mu[None]
            var = jnp.sum(jnp.sum(xc * xc, axis=0), axis=-1, keepdims=True) * inv_c
            y = xc * lax.rsqrt(var + EPS)[None] * lg_ref[...] + lb_ref[...]
            y = y * _sigmoid(y)
            for cb in range(nc):
                o_ref[i * ls + r0:i * ls + r0 + rt2, cb * LANE:(cb + 1) * LANE] = y[cb].astype(o_ref.dtype)

    for i in range(ns):
        for cb in range(nc):
            tail = ucat[i, cb, ls:ls + HIST, :]
            bufout_ref[i, :, cb * LANE:(cb + 1) * LANE] = tail
            ucat[i, cb, 0:HIST, :] = tail


def _conv(z, buf0, w3, cb3, lg3, lb3, *, ns, ls, tiles, za_col, glu_done=False):
    m = z.shape[0]
    nseq, _, c = buf0.shape
    nc = c // LANE
    groups = nseq // ns
    assert groups * tiles * ns * ls == m
    rows = ns * ls
    pad_rows = -(-(HIST + ls) // 8) * 8
    bspec = pl.BlockSpec((ns, HIST, c), lambda g, t: (g, 0, 0))
    const3 = lambda g, t: (0, 0, 0)
    zspecs = [pl.BlockSpec((rows, c), lambda g, t: (g * tiles + t, za_col))]
    if not glu_done:
        zspecs.append(pl.BlockSpec((rows, c), lambda g, t: (g * tiles + t, za_col + 1)))
    return pl.pallas_call(
        functools.partial(_conv_kernel, ns=ns, ls=ls, nc=nc, glu_done=glu_done),
        grid=(groups, tiles),
        in_specs=[
            *zspecs,
            bspec,
            pl.BlockSpec(w3.shape, const3),
            pl.BlockSpec(cb3.shape, const3),
            pl.BlockSpec(lg3.shape, const3),
            pl.BlockSpec(lb3.shape, const3),
        ],
        out_specs=[pl.BlockSpec((rows, c), lambda g, t: (g * tiles + t, 0)), bspec],
        out_shape=[jax.ShapeDtypeStruct((m, c), BF16), jax.ShapeDtypeStruct(buf0.shape, buf0.dtype)],
        scratch_shapes=[pltpu.VMEM((ns, nc, pad_rows, LANE), F32), pltpu.VMEM((ns, nc, ls, LANE), F32)],
        compiler_params=_params("parallel", "arbitrary"),
        name="conv",
    )(*([z] * len(zspecs)), buf0, w3, cb3, lg3, lb3)


def _xattn_kernel(x_ref, g_ref, wq_ref, k_ref, v_ref, wo_ref, *rest, nb, lq, heads, interleaved):
    o_ref = rest[-1]
    x = x_ref[...]
    if len(rest) > 1:
        a_ref, b_ref, wa_ref, wb_ref = rest[:4]
        x = x + _dot(a_ref[...], wa_ref[...]) + _dot(b_ref[...], wb_ref[...])
    q = _dot(_rms(x, g_ref[...]).astype(BF16), wq_ref[...])
    hd = q.shape[1] // heads
    kv_rows = k_ref.shape[0] // heads if interleaved else k_ref.shape[0]
    mem = kv_rows // nb
    scale = hd ** -0.5
    if nb > 1:
        rows = lax.broadcasted_iota(jnp.int32, (nb * lq, nb * mem), 0) // lq
        cols = lax.broadcasted_iota(jnp.int32, (nb * lq, nb * mem), 1) // mem
        own = rows == cols
    outs = []
    for h in range(heads):
        sl = slice(h * hd, (h + 1) * hd)
        if interleaved:
            kh = k_ref[pl.ds(h, kv_rows, stride=heads), :].astype(BF16)
            vh = v_ref[pl.ds(h, kv_rows, stride=heads), :].astype(BF16)
        else:
            kh = k_ref[:, sl].astype(BF16)
            vh = v_ref[:, sl].astype(BF16)
        s = _dot_nt(q[:, sl].astype(BF16), kh) * scale
        if nb > 1:
            s = jnp.where(own, s, -1e30)
        p = jnp.exp(s - jnp.max(s, axis=-1, keepdims=True))
        l = jnp.sum(p, axis=-1, keepdims=True)
        outs.append((_dot(p.astype(BF16), vh) / l).astype(BF16))
    o_ref[...] = x + _dot(jnp.concatenate(outs, axis=1), wo_ref[...])


def _xattn(x, g, wq, k2d, v2d, wo, *, nb, lq, mem, heads, mix=None):
    m, d = x.shape
    xa = wq.shape[1]
    interleaved = k2d.shape[1] != xa
    kv_blk = (nb * mem * heads, xa // heads) if interleaved else (nb * mem, xa)
    rows = nb * lq
    assert m % rows == 0
    steps = m // rows
    nseq = k2d.shape[0] * nb // kv_blk[0]
    tiles = steps * nb // nseq
    assert tiles * nseq == steps * nb and (nb == 1 or tiles == 1)
    kv_map = lambda i: (i // tiles, 0)
    const = lambda i: (0, 0)
    in_specs = [
        pl.BlockSpec((rows, d), lambda i: (i, 0)),
        pl.BlockSpec((1, d), const),
        pl.BlockSpec((d, xa), const),
        pl.BlockSpec(kv_blk, kv_map),
        pl.BlockSpec(kv_blk, kv_map),
        pl.BlockSpec((xa, d), const),
    ]
    args = [x, g, wq, k2d, v2d, wo]
    if mix is not None:
        a, b, w = mix
        ka = a.shape[1]
        assert w.shape == (2 * ka, d) and b.shape == a.shape
        in_specs += [pl.BlockSpec((rows, ka), lambda i: (i, 0)), pl.BlockSpec((rows, ka), lambda i: (i, 0)),
                     pl.BlockSpec((ka, d), const), pl.BlockSpec((ka, d), lambda i: (1, 0))]
        args += [a, b, w, w]
    return pl.pallas_call(
        functools.partial(_xattn_kernel, nb=nb, lq=lq, heads=heads, interleaved=interleaved),
        grid=(steps,),
        in_specs=in_specs,
        out_specs=pl.BlockSpec((rows, d), lambda i: (i, 0)),
        out_shape=jax.ShapeDtypeStruct((m, d), F32),
        compiler_params=_params("parallel"),
        name="xattn" if mix is None else "outproj_xattn",
    )(*args)


def kernel(x_prompt, x_sample, mem_prompt, state_hgrn, state_conv, cache_mem_k, cache_mem_v, norm_ffn1, ffn1_w_gate, ffn1_w_up, ffn1_w_down, norm_mix, w_in, hgrn_lb, hgrn_gnorm, conv_w, conv_b, conv_ln_g, conv_ln_b, w_out, norm_xattn, norm_mem, xattn_wq, xattn_wk, xattn_wv, xattn_wo, norm_ffn2, ffn2_w_gate, ffn2_w_up, ffn2_w_down, norm_final):
    depth = norm_ffn1.shape[0]
    assert depth == 1
    nb_p, seq, d = x_prompt.shape
    nb_s, dseq, _ = x_sample.shape
    mem = mem_prompt.shape[1]
    heads_a, expand, vdim = state_hgrn.shape[2:]
    c_conv = state_conv.shape[-1]
    xa_heads, xa_hd = cache_mem_k.shape[3:]
    xa = xa_heads * xa_hd
    assert expand == LANE and vdim == LANE and c_conv == heads_a * LANE
    za_col = (2 * heads_a * expand + 2 * heads_a * vdim) // c_conv

    l = 0
    row = lambda p: p.reshape(1, -1).astype(F32)
    bf = lambda w: w[l].astype(BF16)
    w_out16 = bf(w_out)
    wq16, wo16 = bf(xattn_wq), bf(xattn_wo)
    n_mix, n_xa, n_mem, n_fin = row(norm_mix[l]), row(norm_xattn[l]), row(norm_mem[l]), row(norm_final)
    lbp = hgrn_lb.astype(F32)
    gnorm = row(hgrn_gnorm[l])
    nc = c_conv // LANE
    chan3 = lambda p: p.reshape(-1, nc, LANE).transpose(1, 0, 2).astype(F32)
    w3, cb3, lg3, lb3 = chan3(conv_w[l]), chan3(conv_b[l]), chan3(conv_ln_g[l]), chan3(conv_ln_b[l])
    ones_d = jnp.ones((1, d), F32)

    def block(x, k2d, v2d, s0, buf0, f1w, w_in_x, f2w, *, hg, cv, xat, tm, tf, tn, fuse_out):
        x = _ffn(x, row(norm_ffn1[l]), *f1w, ones_d, final_norm=False, tf=tf)
        if f1w[0].dtype == F32:
            x, f1w = x
        glu = w_in_x.dtype == BF16 and tn == c_conv
        z = _norm_matmul(x, n_mix, w_in_x, tm=tm, tn=tn, glu_tail=glu)
        if w_in_x.dtype == F32:
            z, w_in_x = z
        o_a, s_new = _hgrn(z, lbp, gnorm, s0, **hg)
        o_b, buf_new = _conv(z, buf0, w3, cb3, lg3, lb3, za_col=za_col, glu_done=glu, **cv)
        if fuse_out:
            x = _xattn(x, n_xa, wq16, k2d, v2d, wo16, mem=mem, heads=xa_heads, mix=(o_a, o_b, w_out16), **xat)
        else:
            x = _outproj(x, o_a, o_b, w_out16, tm=512, tn=d)
            x = _xattn(x, n_xa, wq16, k2d, v2d, wo16, mem=mem, heads=xa_heads, **xat)
        y = _ffn(x, row(norm_ffn2[l]), *f2w, n_fin, final_norm=True, tf=tf)
        if f2w[0].dtype == F32:
            y, f2w = y
        return y, s_new, buf_new, (f1w, w_in_x, f2w)

    ys, shs, scs, w16 = block(
        x_sample.reshape(nb_s * dseq, d),
        cache_mem_k[l].reshape(nb_s * mem * xa_heads, xa_hd),
        cache_mem_v[l].reshape(nb_s * mem * xa_heads, xa_hd),
        state_hgrn[l], state_conv[l],
        (ffn1_w_gate[l], ffn1_w_up[l], ffn1_w_down[l]), w_in[l],
        (ffn2_w_gate[l], ffn2_w_up[l], ffn2_w_down[l]),
        hg=dict(bk=dseq, tiles=1, hs=4),
        cv=dict(ns=8, ls=dseq, tiles=1),
        xat=dict(nb=8, lq=dseq), tm=1024, tf=256, tn=512, fuse_out=False)

    memx = mem_prompt.reshape(nb_p * mem, d)
    mk, mv = _mem_kv(memx, n_mem, xattn_wk[l], xattn_wv[l], tm=256)
    s0p = jnp.zeros((nb_p, heads_a, expand, vdim), state_hgrn.dtype)
    b0p = jnp.zeros((nb_p, HIST, c_conv), state_conv.dtype)
    yp, shp, scp, _ = block(
        x_prompt.reshape(nb_p * seq, d), mk, mv, s0p, b0p, *w16,
        hg=dict(bk=ROWS, tiles=seq // ROWS, hs=heads_a, tps=4),
        cv=dict(ns=1, ls=256, tiles=seq // 256),
        xat=dict(nb=1, lq=512), tm=1024, tf=512, tn=c_conv, fuse_out=True)

    return (yp.reshape(nb_p, seq, d), ys.reshape(nb_s, dseq, d),
            shp[None], scp[None],
            mk.reshape(1, nb_p, mem, xa_heads, xa_hd).astype(cache_mem_k.dtype),
            mv.reshape(1, nb_p, mem, xa_heads, xa_hd).astype(cache_mem_v.dtype),
            shs[None], scs[None])
```

```python
import functools

import numpy as np
import jax
import jax.numpy as jnp
from jax import lax
from jax.experimental import pallas as pl
from jax.experimental.pallas import tpu as pltpu

F32 = jnp.float32
BF16 = jnp.bfloat16
EPS = 1e-6
FFN_RES = 0.5
LANE = 128
ROWS = 128
CONV_W = 31
HIST = CONV_W - 1
VMEM_LIMIT = 56 * 1024 * 1024


def _params(*sem):
    return pltpu.CompilerParams(dimension_semantics=sem, vmem_limit_bytes=VMEM_LIMIT)


def _sigmoid(x):
    return 1.0 / (1.0 + jnp.exp(-x))


def _rms(x, g):
    return x * lax.rsqrt(jnp.mean(x * x, axis=-1, keepdims=True) + EPS) * g


def _dot(a, b):
    return jnp.dot(a, b, preferred_element_type=F32)


def _dot_nt(a, b):
    return lax.dot_general(a, b, (((1,), (1,)), ((), ())), preferred_element_type=F32)


def _dot_tn(a, b):
    return lax.dot_general(a, b, (((0,), (0,)), ((), ())), preferred_element_type=F32)


def _split3(x):
    hi = x.astype(BF16)
    r1 = x - hi.astype(F32)
    mid = r1.astype(BF16)
    lo = (r1 - mid.astype(F32)).astype(BF16)
    return hi, mid, lo


def _ffn_kernel(x_ref, g_ref, wg_ref, wu_ref, wd_ref, gf_ref, o_ref, *rest, final_norm, emit_bf16):
    j = pl.program_id(1)
    xn_ref = rest[-1]
    if emit_bf16:
        for src, dst in zip((wg_ref, wu_ref, wd_ref), rest[:3]):
            dst[...] = src[...].astype(BF16)
        wg_ref, wu_ref, wd_ref = rest[:3]

    slab = 128
    nslab = x_ref.shape[0] // slab

    @pl.when(j == 0)
    def _():
        def body(c, carry):
            r = pl.ds(pl.multiple_of(c * slab, slab), slab)
            x = x_ref[r, :]
            xn_ref[r, :] = _rms(x, g_ref[...]).astype(BF16)
            o_ref[r, :] = x
            return carry
        lax.fori_loop(0, nslab, body, 0)

    xn = xn_ref[...]
    g = _dot(xn, wg_ref[...])
    u = _dot(xn, wu_ref[...])
    h = (FFN_RES * (g * _sigmoid(g)) * u).astype(BF16)
    o_ref[...] += _dot(h, wd_ref[...])

    if final_norm:
        @pl.when(j == pl.num_programs(1) - 1)
        def _():
            def body(c, carry):
                r = pl.ds(pl.multiple_of(c * slab, slab), slab)
                o_ref[r, :] = _rms(o_ref[r, :], gf_ref[...])
                return carry
            lax.fori_loop(0, nslab, body, 0)


def _ffn_piped_kernel(x_ref, g_ref, wg_hbm, wu_hbm, wd_hbm, gf_ref, o_ref, wg16_hbm, wu16_hbm, wd16_hbm,
                      xn_ref, *, final_norm, tf):
    slab = 128
    nslab = x_ref.shape[0] // slab
    d = x_ref.shape[1]

    def prologue(c, carry):
        r = pl.ds(pl.multiple_of(c * slab, slab), slab)
        x = x_ref[r, :]
        xn_ref[r, :] = _rms(x, g_ref[...]).astype(BF16)
        o_ref[r, :] = x
        return carry
    lax.fori_loop(0, nslab, prologue, 0)

    def body(wg_ref, wu_ref, wd_ref, wg16_ref, wu16_ref, wd16_ref):
        for src, dst in ((wg_ref, wg16_ref), (wu_ref, wu16_ref), (wd_ref, wd16_ref)):
            dst[...] = src[...].astype(BF16)
        xn = xn_ref[...]
        gate = _dot(xn, wg16_ref[...])
        up = _dot(xn, wu16_ref[...])
        h = (FFN_RES * (gate * _sigmoid(gate)) * up).astype(BF16)
        o_ref[...] += _dot(h, wd16_ref[...])

    deep = pl.Buffered(3)
    col = lambda j: (0, j)
    rowb = lambda j: (j, 0)
    pltpu.emit_pipeline(
        body, grid=(wg_hbm.shape[1] // tf,),
        in_specs=[pl.BlockSpec((d, tf), col, pipeline_mode=deep),
                  pl.BlockSpec((d, tf), col, pipeline_mode=deep),
                  pl.BlockSpec((tf, d), rowb, pipeline_mode=deep)],
        out_specs=[pl.BlockSpec((d, tf), col), pl.BlockSpec((d, tf), col), pl.BlockSpec((tf, d), rowb)],
    )(wg_hbm, wu_hbm, wd_hbm, wg16_hbm, wu16_hbm, wd16_hbm)

    if final_norm:
        def epilogue(c, carry):
            r = pl.ds(pl.multiple_of(c * slab, slab), slab)
            o_ref[r, :] = _rms(o_ref[r, :], gf_ref[...])
            return carry
        lax.fori_loop(0, nslab, epilogue, 0)


def _ffn_piped(x, g, wg, wu, wd, gf, *, final_norm, tf):
    m, d = x.shape
    assert wg.shape[1] % tf == 0
    vmem = pl.BlockSpec(memory_space=pltpu.VMEM)
    hbm = pl.BlockSpec(memory_space=pl.ANY)
    res = pl.pallas_call(
        functools.partial(_ffn_piped_kernel, final_norm=final_norm, tf=tf),
        in_specs=[vmem, vmem, hbm, hbm, hbm, vmem],
        out_specs=[vmem, hbm, hbm, hbm],
        out_shape=[jax.ShapeDtypeStruct((m, d), F32)] + [jax.ShapeDtypeStruct(w.shape, BF16) for w in (wg, wu, wd)],
        scratch_shapes=[pltpu.VMEM((m, d), BF16)],
        compiler_params=pltpu.CompilerParams(vmem_limit_bytes=VMEM_LIMIT),
        name="ffn_cast",
    )(x, g, wg, wu, wd, gf)
    return res[0], tuple(res[1:])


def _ffn(x, g, wg, wu, wd, gf, *, final_norm, tm=1024, tf=512):
    m, d = x.shape
    f = wg.shape[1]
    emit_bf16 = wg.dtype == F32
    if emit_bf16:
        tm = m
    assert m % tm == 0 and f % tf == 0
    row_mode = dict(pipeline_mode=pl.Buffered(1)) if m == tm else {}
    wspecs = [pl.BlockSpec((d, tf), lambda i, j: (0, j)),
              pl.BlockSpec((d, tf), lambda i, j: (0, j)),
              pl.BlockSpec((tf, d), lambda i, j: (j, 0))]
    if emit_bf16:
        return _ffn_piped(x, g, wg, wu, wd, gf, final_norm=final_norm, tf=tf)
    out_specs =[pl.BlockSpec((tm, d), lambda i, j: (i, 0), **row_mode)]
    out_shape = [jax.ShapeDtypeStruct((m, d), F32)]
    if emit_bf16:
        out_specs += wspecs
        out_shape += [jax.ShapeDtypeStruct(w.shape, BF16) for w in (wg, wu, wd)]
    res = pl.pallas_call(
        functools.partial(_ffn_kernel, final_norm=final_norm, emit_bf16=emit_bf16),
        grid=(m // tm, f // tf),
        in_specs=[
            pl.BlockSpec((tm, d), lambda i, j: (i, 0), **row_mode),
            pl.BlockSpec((1, d), lambda i, j: (0, 0)),
            *wspecs,
            pl.BlockSpec((1, d), lambda i, j: (0, 0)),
        ],
        out_specs=out_specs,
        out_shape=out_shape,
        scratch_shapes=[pltpu.VMEM((tm, d), BF16)],
        compiler_params=_params("parallel", "arbitrary"),
        name="ffn_cast" if emit_bf16 else "ffn",
    )(x, g, wg, wu, wd, gf)
    return (res[0], tuple(res[1:])) if emit_bf16 else res[0]


def _norm_matmul_kernel(x_ref, g_ref, w_ref, o_ref, *rest, emit_bf16):
    xn_ref = rest[-1]
    if emit_bf16:
        rest[0][...] = w_ref[...].astype(BF16)
        w_ref = rest[0]

    @pl.when(pl.program_id(1) == 0)
    def _():
        xn_ref[...] = _rms(x_ref[...], g_ref[...]).astype(BF16)

    o_ref[...] = _dot(xn_ref[...], w_ref[...])


def _norm_matmul(x, g, w, *, tm, tn):
    m, d = x.shape
    n = w.shape[1]
    emit_bf16 = w.dtype == F32
    if emit_bf16:
        tm = m
    assert m % tm == 0 and n % tn == 0
    wspec = pl.BlockSpec((d, tn), lambda i, j: (0, j))
    out_specs = [pl.BlockSpec((tm, tn), lambda i, j: (i, j))]
    out_shape = [jax.ShapeDtypeStruct((m, n), F32)]
    if emit_bf16:
        out_specs.append(wspec)
        out_shape.append(jax.ShapeDtypeStruct(w.shape, BF16))
    res = pl.pallas_call(
        functools.partial(_norm_matmul_kernel, emit_bf16=emit_bf16),
        grid=(m // tm, n // tn),
        in_specs=[
            pl.BlockSpec((tm, d), lambda i, j: (i, 0)),
            pl.BlockSpec((1, d), lambda i, j: (0, 0)),
            wspec,
        ],
        out_specs=out_specs,
        out_shape=out_shape,
        scratch_shapes=[pltpu.VMEM((tm, d), BF16)],
        compiler_params=_params("parallel", "arbitrary"),
        name="norm_matmul_cast" if emit_bf16 else "norm_matmul",
    )(x, g, w)
    return tuple(res) if emit_bf16 else res[0]


def _mem_kv_kernel(x_ref, g_ref, wk_ref, wv_ref, k_ref, v_ref):
    xn = _rms(x_ref[...], g_ref[...]).astype(BF16)
    k_ref[...] = _dot(xn, wk_ref[...].astype(BF16))
    v_ref[...] = _dot(xn, wv_ref[...].astype(BF16))


def _mem_kv(x, g, wk, wv, *, tm):
    m, d = x.shape
    n = wk.shape[1]
    assert m % tm == 0 and wv.shape == wk.shape
    const = lambda i: (0, 0)
    ospec = pl.BlockSpec((tm, n), lambda i: (i, 0))
    return pl.pallas_call(
        _mem_kv_kernel,
        grid=(m // tm,),
        in_specs=[pl.BlockSpec((tm, d), lambda i: (i, 0)), pl.BlockSpec((1, d), const),
                  pl.BlockSpec((d, n), const), pl.BlockSpec((d, n), const)],
        out_specs=[ospec, ospec],
        out_shape=[jax.ShapeDtypeStruct((m, n), F32)] * 2,
        compiler_params=_params("parallel"),
        name="mem_kv",
    )(x, g, wk, wv)


def _outproj_kernel(x_ref, a_ref, b_ref, wa_ref, wb_ref, o_ref):
    o_ref[...] = x_ref[...] + _dot(a_ref[...], wa_ref[...]) + _dot(b_ref[...], wb_ref[...])


def _outproj(x, a, b, w, *, tm, tn):
    m, d = x.shape
    ka = a.shape[1]
    assert w.shape[0] == 2 * ka and b.shape[1] == ka and m % tm == 0 and d % tn == 0
    return pl.pallas_call(
        _outproj_kernel,
        grid=(m // tm, d // tn),
        in_specs=[
            pl.BlockSpec((tm, tn), lambda i, j: (i, j)),
            pl.BlockSpec((tm, ka), lambda i, j: (i, 0)),
            pl.BlockSpec((tm, ka), lambda i, j: (i, 0)),
            pl.BlockSpec((ka, tn), lambda i, j: (0, j)),
            pl.BlockSpec((ka, tn), lambda i, j: (1, j)),
        ],
        out_specs=pl.BlockSpec((tm, tn), lambda i, j: (i, j)),
        out_shape=jax.ShapeDtypeStruct((m, d), F32),
        compiler_params=_params("parallel", "arbitrary"),
        name="outproj",
    )(x, a, b, w, w)


def _hgrn_structure(bk):
    r = np.arange(ROWS)
    blk = r // bk
    tri = (blk[:, None] == blk[None, :]) & (r[None, :] <= r[:, None])
    masks = [np.eye(ROWS, dtype=bool)]
    m = 1
    while m < bk:
        grp = r // (2 * m)
        is_r = (r % (2 * m)) >= m
        masks.append((grp[:, None] == grp[None, :]) & is_r[:, None] & (~is_r)[None, :])
        m *= 2
    nblk = ROWS // bk
    rsel = blk[:, None] == (np.arange(nblk * LANE) // LANE)[None, :]
    return jnp.asarray(tri, BF16), jnp.asarray(np.stack(masks), F32), jnp.asarray(rsel, F32), nblk


def _hgrn_gates(zf_ref, lb_ref, tri_ref):
    lbr = lb_ref[...]
    e = jnp.exp(lbr - jnp.max(lbr, axis=0, keepdims=True))
    lb = e[0:1] / jnp.sum(e, axis=0, keepdims=True)
    f = lb + (1.0 - lb) * _sigmoid(zf_ref[...])
    parts = _split3(jnp.log2(f))
    tri = tri_ref[...]
    b = _dot(tri, parts[0]) + _dot(tri, parts[1]) + _dot(tri, parts[2])
    return f, parts, b


def _level_signs(row, bk):
    sgns = []
    m = 2
    while m < bk:
        sgns.append(jnp.where((row & (2 * m - 1)) >= m, 1.0, -1.0))
        m *= 2
    return sgns


def _level_weights(f, b, row, sgns, bk):
    width = b.shape[1]
    ws = [jnp.where((row & 1) == 1, f, 1.0)]
    m = 2
    while m < bk:
        g = max(2 * m, 8)
        bg = b.reshape(ROWS // g, g, width)
        if 2 * m >= 8:
            ref = bg[:, m - 1:m, :]
        else:
            sub = row.reshape(ROWS // 8, 8, width) & 7
            ref = jnp.where(sub < 4, bg[:, 1:2, :], bg[:, 5:6, :])
        ws.append(jnp.exp2((bg - ref).reshape(ROWS, width) * sgns[len(ws) - 1]))
        m *= 2
    return ws


def _intra_scores(q, k, ws, mask_ref):
    a = mask_ref[0] * _dot_nt(q.astype(BF16), k.astype(BF16))
    for l, w in enumerate(ws):
        a = a + mask_ref[1 + l] * _dot_nt((q * w).astype(BF16), (k * w).astype(BF16))
    return a.astype(BF16)


def _hgrn_seq_kernel(zq_ref, zf_ref, zi_ref, zg_ref, lb_ref, gn_ref, tri_ref, mask_ref, s0_ref,
                     o_ref, sout_ref, st_scr, *, heads):
    t = pl.program_id(1)

    @pl.when(t == 0)
    def _():
        for h in range(heads):
            st_scr[h] = s0_ref[0, h].T

    row = lax.broadcasted_iota(jnp.int32, (ROWS, LANE), 0)
    sgns = _level_signs(row, ROWS)
    for tt in range(zq_ref.shape[0] // ROWS):
        rs = slice(tt * ROWS, (tt + 1) * ROWS)
        f_all, _, b_all = _hgrn_gates(zf_ref.at[rs, :], lb_ref, tri_ref)
        for h in range(heads):
            sl = slice(h * LANE, (h + 1) * LANE)
            zq = zq_ref[rs, sl]
            q = zq * _sigmoid(zq)
            f = f_all[:, sl]
            k = 1.0 - f
            b = b_all[:, sl]
            a = _intra_scores(q, k, _level_weights(f, b, row, sgns, ROWS), mask_ref)
            v16 = zi_ref[rs, sl].astype(BF16)
            st = st_scr[h]
            bend = b[ROWS - 1:ROWS, :]
            o = _dot(a, v16) + _dot_nt((q * jnp.exp2(b)).astype(BF16), st.astype(BF16))
            kc = (k * jnp.exp2(bend - b)).astype(BF16)
            st_scr[h] = jnp.exp2(bend) * st + _dot_tn(v16, kc)
            zg = zg_ref[rs, sl]
            o_ref[rs, sl] = (_rms(o, gn_ref[...]) * (zg * _sigmoid(zg))).astype(o_ref.dtype)

    @pl.when(t == pl.num_programs(1) - 1)
    def _():
        for h in range(heads):
            sout_ref[0, h] = st_scr[h].T


def _hgrn_blocks_kernel(zq_ref, zf_ref, zi_ref, zg_ref, lb_ref, gn_ref, tri_ref, mask_ref, rsel_ref, s0_ref,
                        o_ref, sout_ref, *, heads, bk, nblk):
    f_all, parts, b_all = _hgrn_gates(zf_ref, lb_ref, tri_ref)
    row = lax.broadcasted_iota(jnp.int32, (ROWS, LANE), 0)
    sgns = _level_signs(row, bk)
    rsel = rsel_ref[...]
    rs16 = rsel.astype(BF16)
    rs16x3 = jnp.concatenate([rs16] * 3, axis=0)
    for h in range(heads):
        sl = slice(h * LANE, (h + 1) * LANE)
        zq = zq_ref[:, sl]
        q = zq * _sigmoid(zq)
        f = f_all[:, sl]
        k = 1.0 - f
        b = b_all[:, sl]
        a = _intra_scores(q, k, _level_weights(f, b, row, sgns, bk), mask_ref)
        v = zi_ref[:, sl]
        scat = jnp.concatenate([s0_ref[i, h] for i in range(nblk)], axis=1)
        y = _dot((q * jnp.exp2(b)).astype(BF16), scat.astype(BF16))
        o = _dot(a, v.astype(BF16)) + jnp.concatenate(
            [y[i * bk:(i + 1) * bk, i * LANE:(i + 1) * LANE] for i in range(nblk)], axis=0)
        bg = b.reshape(nblk, bk, LANE)
        c = (bg[:, bk - 1:bk, :] - bg).reshape(ROWS, LANE)
        kc = (k * jnp.exp2(c)).astype(BF16)
        vblk = (jnp.concatenate([v] * nblk, axis=1) * rsel).astype(BF16)
        bend_t = _dot_tn(jnp.concatenate([p[:, sl] for p in parts], axis=0), rs16x3)
        snew = jnp.exp2(bend_t) * scat + _dot_tn(kc, vblk)
        for i in range(nblk):
            sout_ref[i, h] = snew[:, i * LANE:(i + 1) * LANE]
        zg = zg_ref[:, sl]
        o_ref[:, sl] = (_rms(o, gn_ref[...]) * (zg * _sigmoid(zg))).astype(o_ref.dtype)


def _hgrn(z, lbp, gnorm, s0, *, bk, tiles, hs, tps=1):
    m = z.shape[0]
    nseq, heads = s0.shape[0], s0.shape[1]
    tri, masks, rsel, nblk = _hgrn_structure(bk)
    groups = m // (ROWS * tiles)
    assert groups * nblk == nseq and groups * tiles * ROWS == m and heads % hs == 0 and tiles % tps == 0
    steps = tiles // tps
    hb = heads // hs
    w = hs * LANE
    seq_mode = nblk == 1

    def zspec(part):
        if seq_mode:
            return pl.BlockSpec((tps * ROWS, w), lambda g, j: (g * steps + j, part))
        return pl.BlockSpec((ROWS, w), lambda g, j: (g, part * hb + j))

    const2 = lambda g, j: (0, 0)
    const3 = lambda g, j: (0, 0, 0)
    common = [zspec(0), zspec(1), zspec(2), zspec(3),
              pl.BlockSpec((lbp.shape[0], w), const2 if seq_mode else (lambda g, j: (0, j))),
              pl.BlockSpec((1, LANE), const2),
              pl.BlockSpec(tri.shape, const2),
              pl.BlockSpec(masks.shape, const3)]
    out_shape = [jax.ShapeDtypeStruct((m, heads * LANE), BF16), jax.ShapeDtypeStruct(s0.shape, s0.dtype)]
    if seq_mode:
        assert hs == heads
        sspec = pl.BlockSpec((1, heads, LANE, LANE), lambda g, j: (g, 0, 0, 0))
        return pl.pallas_call(
            functools.partial(_hgrn_seq_kernel, heads=heads),
            grid=(groups, steps),
            in_specs=common + [sspec],
            out_specs=[pl.BlockSpec((tps * ROWS, w), lambda g, j: (g * steps + j, 0)), sspec],
            out_shape=out_shape,
            scratch_shapes=[pltpu.VMEM((heads, LANE, LANE), F32)],
            compiler_params=_params("parallel", "arbitrary"),
            name="hgrn_seq",
        )(z, z, z, z, lbp, gnorm, tri, masks, s0)
    assert tiles == 1
    sspec = pl.BlockSpec((nblk, hs, LANE, LANE), lambda g, j: (g, j, 0, 0))
    return pl.pallas_call(
        functools.partial(_hgrn_blocks_kernel, heads=hs, bk=bk, nblk=nblk),
        grid=(groups, hb),
        in_specs=common + [pl.BlockSpec(rsel.shape, const2), sspec],
        out_specs=[pl.BlockSpec((ROWS, w), lambda g, j: (g, j)), sspec],
        out_shape=out_shape,
        compiler_params=_params("parallel", "parallel"),
        name="hgrn_blocks",
    )(z, z, z, z, lbp, gnorm, tri, masks, rsel, s0)


def _conv_kernel(za_ref, zb_ref, buf_ref, w_ref, cb_ref, lg_ref, lb_ref, o_ref, bufout_ref, ucat, dw,
                 *, ns, ls, nc):
    u = za_ref[...] * _sigmoid(zb_ref[...])

    @pl.when(pl.program_id(1) == 0)
    def _():
        for i in range(ns):
            for cb in range(nc):
                ucat[i, cb, 0:HIST, :] = buf_ref[i, :, cb * LANE:(cb + 1) * LANE]

    for i in range(ns):
        for cb in range(nc):
            ucat[i, cb, HIST:HIST + ls, :] = u[i * ls:(i + 1) * ls, cb * LANE:(cb + 1) * LANE]

    rt = min(ls, 64)

    def cb_body(cb, carry):
        for i in range(ns):
            for r0 in range(0, ls, rt):
                acc = jnp.broadcast_to(cb_ref[cb], (rt, LANE))
                for j in range(CONV_W):
                    acc = acc + ucat[i, cb, r0 + j:r0 + j + rt, :] * w_ref[cb, j:j + 1, :]
                dw[i, cb, r0:r0 + rt, :] = acc
        return carry

    lax.fori_loop(0, nc, cb_body, 0)

    rt2 = min(ls, 32)
    inv_c = 1.0 / (nc * LANE)
    for i in range(ns):
        for r0 in range(0, ls, rt2):
            x = dw[i, :, r0:r0 + rt2, :]
            mu = jnp.sum(jnp.sum(x, axis=0), axis=-1, keepdims=True) * inv_c
            xc = x - mu[None]
            var = jnp.sum(jnp.sum(xc * xc, axis=0), axis=-1, keepdims=True) * inv_c
            y = xc * lax.rsqrt(var + EPS)[None] * lg_ref[...] + lb_ref[...]
            y = y * _sigmoid(y)
            for cb in range(nc):
                o_ref[i * ls + r0:i * ls + r0 + rt2, cb * LANE:(cb + 1) * LANE] = y[cb].astype(o_ref.dtype)

    for i in range(ns):
        for cb in range(nc):
            tail = ucat[i, cb, ls:ls + HIST, :]
            bufout_ref[i, :, cb * LANE:(cb + 1) * LANE] = tail
            ucat[i, cb, 0:HIST, :] = tail


def _conv(z, buf0, w3, cb3, lg3, lb3, *, ns, ls, tiles, za_col):
    m = z.shape[0]
    nseq, _, c = buf0.shape
    nc = c // LANE
    groups = nseq // ns
    assert groups * tiles * ns * ls == m
    rows = ns * ls
    pad_rows = -(-(HIST + ls) // 8) * 8
    bspec = pl.BlockSpec((ns, HIST, c), lambda g, t: (g, 0, 0))
    const3 = lambda g, t: (0, 0, 0)
    return pl.pallas_call(
        functools.partial(_conv_kernel, ns=ns, ls=ls, nc=nc),
        grid=(groups, tiles),
        in_specs=[
            pl.BlockSpec((rows, c), lambda g, t: (g * tiles + t, za_col)),
            pl.BlockSpec((rows, c), lambda g, t: (g * tiles + t, za_col + 1)),
            bspec,
            pl.BlockSpec(w3.shape, const3),
            pl.BlockSpec(cb3.shape, const3),
            pl.BlockSpec(lg3.shape, const3),
            pl.BlockSpec(lb3.shape, const3),
        ],
        out_specs=[pl.BlockSpec((rows, c), lambda g, t: (g * tiles + t, 0)), bspec],
        out_shape=[jax.ShapeDtypeStruct((m, c), BF16), jax.ShapeDtypeStruct(buf0.shape, buf0.dtype)],
        scratch_shapes=[pltpu.VMEM((ns, nc, pad_rows, LANE), F32), pltpu.VMEM((ns, nc, ls, LANE), F32)],
        compiler_params=_params("parallel", "arbitrary"),
        name="conv",
    )(z, z, buf0, w3, cb3, lg3, lb3)


def _xattn_kernel(x_ref, g_ref, wq_ref, k_ref, v_ref, wo_ref, *rest, nb, lq, heads, interleaved):
    o_ref = rest[-1]
    x = x_ref[...]
    if len(rest) > 1:
        a_ref, b_ref, wa_ref, wb_ref = rest[:4]
        x = x + _dot(a_ref[...], wa_ref[...]) + _dot(b_ref[...], wb_ref[...])
    q = _dot(_rms(x, g_ref[...]).astype(BF16), wq_ref[...])
    hd = q.shape[1] // heads
    kv_rows = k_ref.shape[0] // heads if interleaved else k_ref.shape[0]
    mem = kv_rows // nb
    scale = hd ** -0.5
    if nb > 1:
        rows = lax.broadcasted_iota(jnp.int32, (nb * lq, nb * mem), 0) // lq
        cols = lax.broadcasted_iota(jnp.int32, (nb * lq, nb * mem), 1) // mem
        own = rows == cols
    outs = []
    for h in range(heads):
        sl = slice(h * hd, (h + 1) * hd)
        if interleaved:
            kh = k_ref[pl.ds(h, kv_rows, stride=heads), :].astype(BF16)
            vh = v_ref[pl.ds(h, kv_rows, stride=heads), :].astype(BF16)
        else:
            kh = k_ref[:, sl].astype(BF16)
            vh = v_ref[:, sl].astype(BF16)
        s = _dot_nt(q[:, sl].astype(BF16), kh) * scale
        if nb > 1:
            s = jnp.where(own, s, -1e30)
        p = jnp.exp(s - jnp.max(s, axis=-1, keepdims=True))
        l = jnp.sum(p, axis=-1, keepdims=True)
        outs.append((_dot(p.astype(BF16), vh) / l).astype(BF16))
    o_ref[...] = x + _dot(jnp.concatenate(outs, axis=1), wo_ref[...])


def _xattn(x, g, wq, k2d, v2d, wo, *, nb, lq, mem, heads, mix=None):
    m, d = x.shape
    xa = wq.shape[1]
    interleaved = k2d.shape[1] != xa
    kv_blk = (nb * mem * heads, xa // heads) if interleaved else (nb * mem, xa)
    rows = nb * lq
    assert m % rows == 0
    steps = m // rows
    nseq = k2d.shape[0] * nb // kv_blk[0]
    tiles = steps * nb // nseq
    assert tiles * nseq == steps * nb and (nb == 1 or tiles == 1)
    kv_map = lambda i: (i // tiles, 0)
    const = lambda i: (0, 0)
    in_specs = [
        pl.BlockSpec((rows, d), lambda i: (i, 0)),
        pl.BlockSpec((1, d), const),
        pl.BlockSpec((d, xa), const),
        pl.BlockSpec(kv_blk, kv_map),
        pl.BlockSpec(kv_blk, kv_map),
        pl.BlockSpec((xa, d), const),
    ]
    args = [x, g, wq, k2d, v2d, wo]
    if mix is not None:
        a, b, w = mix
        ka = a.shape[1]
        assert w.shape == (2 * ka, d) and b.shape == a.shape
        in_specs += [pl.BlockSpec((rows, ka), lambda i: (i, 0)), pl.BlockSpec((rows, ka), lambda i: (i, 0)),
                     pl.BlockSpec((ka, d), const), pl.BlockSpec((ka, d), lambda i: (1, 0))]
        args += [a, b, w, w]
    return pl.pallas_call(
        functools.partial(_xattn_kernel, nb=nb, lq=lq, heads=heads, interleaved=interleaved),
        grid=(steps,),
        in_specs=in_specs,
        out_specs=pl.BlockSpec((rows, d), lambda i: (i, 0)),
        out_shape=jax.ShapeDtypeStruct((m, d), F32),
        compiler_params=_params("parallel"),
        name="xattn" if mix is None else "outproj_xattn",
    )(*args)


def kernel(x_prompt, x_sample, mem_prompt, state_hgrn, state_conv, cache_mem_k, cache_mem_v, norm_ffn1, ffn1_w_gate, ffn1_w_up, ffn1_w_down, norm_mix, w_in, hgrn_lb, hgrn_gnorm, conv_w, conv_b, conv_ln_g, conv_ln_b, w_out, norm_xattn, norm_mem, xattn_wq, xattn_wk, xattn_wv, xattn_wo, norm_ffn2, ffn2_w_gate, ffn2_w_up, ffn2_w_down, norm_final):
    depth = norm_ffn1.shape[0]
    assert depth == 1
    nb_p, seq, d = x_prompt.shape
    nb_s, dseq, _ = x_sample.shape
    mem = mem_prompt.shape[1]
    heads_a, expand, vdim = state_hgrn.shape[2:]
    c_conv = state_conv.shape[-1]
    xa_heads, xa_hd = cache_mem_k.shape[3:]
    xa = xa_heads * xa_hd
    assert expand == LANE and vdim == LANE and c_conv == heads_a * LANE
    za_col = (2 * heads_a * expand + 2 * heads_a * vdim) // c_conv

    l = 0
    row = lambda p: p.reshape(1, -1).astype(F32)
    bf = lambda w: w[l].astype(BF16)
    w_out16 = bf(w_out)
    wq16, wo16 = bf(xattn_wq), bf(xattn_wo)
    n_mix, n_xa, n_mem, n_fin = row(norm_mix[l]), row(norm_xattn[l]), row(norm_mem[l]), row(norm_final)
    lbp = hgrn_lb.astype(F32)
    gnorm = row(hgrn_gnorm[l])
    nc = c_conv // LANE
    chan3 = lambda p: p.reshape(-1, nc, LANE).transpose(1, 0, 2).astype(F32)
    w3, cb3, lg3, lb3 = chan3(conv_w[l]), chan3(conv_b[l]), chan3(conv_ln_g[l]), chan3(conv_ln_b[l])
    ones_d = jnp.ones((1, d), F32)

    def block(x, k2d, v2d, s0, buf0, f1w, w_in_x, f2w, *, hg, cv, xat, tm, tf, tn, fuse_out):
        x = _ffn(x, row(norm_ffn1[l]), *f1w, ones_d, final_norm=False, tf=tf)
        if f1w[0].dtype == F32:
            x, f1w = x
        z = _norm_matmul(x, n_mix, w_in_x, tm=tm, tn=tn)
        if w_in_x.dtype == F32:
            z, w_in_x = z
        o_a, s_new = _hgrn(z, lbp, gnorm, s0, **hg)
        o_b, buf_new = _conv(z, buf0, w3, cb3, lg3, lb3, za_col=za_col, **cv)
        if fuse_out:
            x = _xattn(x, n_xa, wq16, k2d, v2d, wo16, mem=mem, heads=xa_heads, mix=(o_a, o_b, w_out16), **xat)
        else:
            x = _outproj(x, o_a, o_b, w_out16, tm=512, tn=d)
            x = _xattn(x, n_xa, wq16, k2d, v2d, wo16, mem=mem, heads=xa_heads, **xat)
        y = _ffn(x, row(norm_ffn2[l]), *f2w, n_fin, final_norm=True, tf=tf)
        if f2w[0].dtype == F32:
            y, f2w = y
        return y, s_new, buf_new, (f1w, w_in_x, f2w)

    ys, shs, scs, w16 = block(
        x_sample.reshape(nb_s * dseq, d),
        cache_mem_k[l].reshape(nb_s * mem * xa_heads, xa_hd),
        cache_mem_v[l].reshape(nb_s * mem * xa_heads, xa_hd),
        state_hgrn[l], state_conv[l],
        (ffn1_w_gate[l], ffn1_w_up[l], ffn1_w_down[l]), w_in[l],
        (ffn2_w_gate[l], ffn2_w_up[l], ffn2_w_down[l]),
        hg=dict(bk=dseq, tiles=1, hs=4),
        cv=dict(ns=8, ls=dseq, tiles=1),
        xat=dict(nb=8, lq=dseq), tm=1024, tf=256, tn=512, fuse_out=False)

    memx = mem_prompt.reshape(nb_p * mem, d)
    mk, mv = _mem_kv(memx, n_mem, xattn_wk[l], xattn_wv[l], tm=256)
    s0p = jnp.zeros((nb_p, heads_a, expand, vdim), state_hgrn.dtype)
    b0p = jnp.zeros((nb_p, HIST, c_conv), state_conv.dtype)
    yp, shp, scp, _ = block(
        x_prompt.reshape(nb_p * seq, d), mk, mv, s0p, b0p, *w16,
        hg=dict(bk=ROWS, tiles=seq // ROWS, hs=heads_a, tps=4),
        cv=dict(ns=1, ls=256, tiles=seq // 256),
        xat=dict(nb=1, lq=512), tm=1024, tf=512, tn=1536, fuse_out=True)

    return (yp.reshape(nb_p, seq, d), ys.reshape(nb_s, dseq, d),
            shp[None], scp[None],
            mk.reshape(1, nb_p, mem, xa_heads, xa_hd).astype(cache_mem_k.dtype),
            mv.reshape(1, nb_p, mem, xa_heads, xa_hd).astype(cache_mem_v.dtype),
            shs[None], scs[None])
```

```python
import functools

import numpy as np
import jax
import jax.numpy as jnp
from jax import lax
from jax.experimental import pallas as pl
from jax.experimental.pallas import tpu as pltpu

F32 = jnp.float32
BF16 = jnp.bfloat16
EPS = 1e-6
FFN_RES = 0.5
LANE = 128
ROWS = 128
CONV_W = 31
HIST = CONV_W - 1
VMEM_LIMIT = 56 * 1024 * 1024


def _params(*sem):
    return pltpu.CompilerParams(dimension_semantics=sem, vmem_limit_bytes=VMEM_LIMIT)


def _sigmoid(x):
    return 1.0 / (1.0 + jnp.exp(-x))


def _rms(x, g):
    return x * lax.rsqrt(jnp.mean(x * x, axis=-1, keepdims=True) + EPS) * g


def _dot(a, b):
    return jnp.dot(a, b, preferred_element_type=F32)


def _dot_nt(a, b):
    return lax.dot_general(a, b, (((1,), (1,)), ((), ())), preferred_element_type=F32)


def _dot_tn(a, b):
    return lax.dot_general(a, b, (((0,), (0,)), ((), ())), preferred_element_type=F32)


def _split3(x):
    hi = x.astype(BF16)
    r1 = x - hi.astype(F32)
    mid = r1.astype(BF16)
    lo = (r1 - mid.astype(F32)).astype(BF16)
    return hi, mid, lo


def _ffn_kernel(x_ref, g_ref, wg_ref, wu_ref, wd_ref, gf_ref, o_ref, *rest, final_norm, emit_bf16):
    j = pl.program_id(1)
    xn_ref = rest[-1]
    if emit_bf16:
        for src, dst in zip((wg_ref, wu_ref, wd_ref), rest[:3]):
            dst[...] = src[...].astype(BF16)
        wg_ref, wu_ref, wd_ref = rest[:3]

    slab = 128
    nslab = x_ref.shape[0] // slab

    @pl.when(j == 0)
    def _():
        def body(c, carry):
            r = pl.ds(pl.multiple_of(c * slab, slab), slab)
            x = x_ref[r, :]
            xn_ref[r, :] = _rms(x, g_ref[...]).astype(BF16)
            o_ref[r, :] = x
            return carry
        lax.fori_loop(0, nslab, body, 0)

    xn = xn_ref[...]
    g = _dot(xn, wg_ref[...])
    u = _dot(xn, wu_ref[...])
    h = (FFN_RES * (g * _sigmoid(g)) * u).astype(BF16)
    o_ref[...] += _dot(h, wd_ref[...])

    if final_norm:
        @pl.when(j == pl.num_programs(1) - 1)
        def _():
            def body(c, carry):
                r = pl.ds(pl.multiple_of(c * slab, slab), slab)
                o_ref[r, :] = _rms(o_ref[r, :], gf_ref[...])
                return carry
            lax.fori_loop(0, nslab, body, 0)


def _ffn_piped_kernel(x_ref, g_ref, wg_hbm, wu_hbm, wd_hbm, gf_ref, o_ref, wg16_hbm, wu16_hbm, wd16_hbm,
                      xn_ref, *, final_norm, tf):
    slab = 128
    nslab = x_ref.shape[0] // slab
    d = x_ref.shape[1]

    def prologue(c, carry):
        r = pl.ds(pl.multiple_of(c * slab, slab), slab)
        x = x_ref[r, :]
        xn_ref[r, :] = _rms(x, g_ref[...]).astype(BF16)
        o_ref[r, :] = x
        return carry
    lax.fori_loop(0, nslab, prologue, 0)

    def body(wg_ref, wu_ref, wd_ref, wg16_ref, wu16_ref, wd16_ref):
        for src, dst in ((wg_ref, wg16_ref), (wu_ref, wu16_ref), (wd_ref, wd16_ref)):
            dst[...] = src[...].astype(BF16)
        xn = xn_ref[...]
        gate = _dot(xn, wg16_ref[...])
        up = _dot(xn, wu16_ref[...])
        h = (FFN_RES * (gate * _sigmoid(gate)) * up).astype(BF16)
        o_ref[...] += _dot(h, wd16_ref[...])

    deep = pl.Buffered(3)
    col = lambda j: (0, j)
    rowb = lambda j: (j, 0)
    pltpu.emit_pipeline(
        body, grid=(wg_hbm.shape[1] // tf,),
        in_specs=[pl.BlockSpec((d, tf), col, pipeline_mode=deep),
                  pl.BlockSpec((d, tf), col, pipeline_mode=deep),
                  pl.BlockSpec((tf, d), rowb, pipeline_mode=deep)],
        out_specs=[pl.BlockSpec((d, tf), col), pl.BlockSpec((d, tf), col), pl.BlockSpec((tf, d), rowb)],
    )(wg_hbm, wu_hbm, wd_hbm, wg16_hbm, wu16_hbm, wd16_hbm)

    if final_norm:
        def epilogue(c, carry):
            r = pl.ds(pl.multiple_of(c * slab, slab), slab)
            o_ref[r, :] = _rms(o_ref[r, :], gf_ref[...])
            return carry
        lax.fori_loop(0, nslab, epilogue, 0)


def _ffn_piped(x, g, wg, wu, wd, gf, *, final_norm, tf):
    m, d = x.shape
    assert wg.shape[1] % tf == 0
    vmem = pl.BlockSpec(memory_space=pltpu.VMEM)
    hbm = pl.BlockSpec(memory_space=pl.ANY)
    res = pl.pallas_call(
        functools.partial(_ffn_piped_kernel, final_norm=final_norm, tf=tf),
        in_specs=[vmem, vmem, hbm, hbm, hbm, vmem],
        out_specs=[vmem, hbm, hbm, hbm],
        out_shape=[jax.ShapeDtypeStruct((m, d), F32)] + [jax.ShapeDtypeStruct(w.shape, BF16) for w in (wg, wu, wd)],
        scratch_shapes=[pltpu.VMEM((m, d), BF16)],
        compiler_params=pltpu.CompilerParams(vmem_limit_bytes=VMEM_LIMIT),
        name="ffn_cast",
    )(x, g, wg, wu, wd, gf)
    return res[0], tuple(res[1:])


def _ffn(x, g, wg, wu, wd, gf, *, final_norm, tm=1024, tf=512):
    m, d = x.shape
    f = wg.shape[1]
    emit_bf16 = wg.dtype == F32
    if emit_bf16:
        tm = m
    assert m % tm == 0 and f % tf == 0
    row_mode = dict(pipeline_mode=pl.Buffered(1)) if m == tm else {}
    wspecs = [pl.BlockSpec((d, tf), lambda i, j: (0, j)),
              pl.BlockSpec((d, tf), lambda i, j: (0, j)),
              pl.BlockSpec((tf, d), lambda i, j: (j, 0))]
    if emit_bf16:
        return _ffn_piped(x, g, wg, wu, wd, gf, final_norm=final_norm, tf=tf)
    out_specs =[pl.BlockSpec((tm, d), lambda i, j: (i, 0), **row_mode)]
    out_shape = [jax.ShapeDtypeStruct((m, d), F32)]
    if emit_bf16:
        out_specs += wspecs
        out_shape += [jax.ShapeDtypeStruct(w.shape, BF16) for w in (wg, wu, wd)]
    res = pl.pallas_call(
        functools.partial(_ffn_kernel, final_norm=final_norm, emit_bf16=emit_bf16),
        grid=(m // tm, f // tf),
        in_specs=[
            pl.BlockSpec((tm, d), lambda i, j: (i, 0), **row_mode),
            pl.BlockSpec((1, d), lambda i, j: (0, 0)),
            *wspecs,
            pl.BlockSpec((1, d), lambda i, j: (0, 0)),
        ],
        out_specs=out_specs,
        out_shape=out_shape,
        scratch_shapes=[pltpu.VMEM((tm, d), BF16)],
        compiler_params=_params("parallel", "arbitrary"),
        name="ffn_cast" if emit_bf16 else "ffn",
    )(x, g, wg, wu, wd, gf)
    return (res[0], tuple(res[1:])) if emit_bf16 else res[0]


def _norm_matmul_kernel(x_ref, g_ref, w_ref, o_ref, *rest, emit_bf16):
    xn_ref = rest[-1]
    if emit_bf16:
        rest[0][...] = w_ref[...].astype(BF16)
        w_ref = rest[0]

    @pl.when(pl.program_id(1) == 0)
    def _():
        xn_ref[...] = _rms(x_ref[...], g_ref[...]).astype(BF16)

    o_ref[...] = _dot(xn_ref[...], w_ref[...])


def _norm_matmul_piped_kernel(x_ref, g_ref, w_hbm, o_hbm, w16_hbm, xn_ref, *, tn):
    xn_ref[...] = _rms(x_ref[...], g_ref[...]).astype(BF16)
    m, d = x_ref.shape

    def body(w_ref, o_ref, w16_ref):
        w16_ref[...] = w_ref[...].astype(BF16)
        o_ref[...] = _dot(xn_ref[...], w16_ref[...])

    col = lambda j: (0, j)
    pltpu.emit_pipeline(
        body, grid=(w_hbm.shape[1] // tn,),
        in_specs=[pl.BlockSpec((d, tn), col, pipeline_mode=pl.Buffered(3))],
        out_specs=[pl.BlockSpec((m, tn), col), pl.BlockSpec((d, tn), col)],
    )(w_hbm, o_hbm, w16_hbm)


def _norm_matmul_piped(x, g, w, *, tn):
    m, d = x.shape
    n = w.shape[1]
    assert n % tn == 0
    vmem = pl.BlockSpec(memory_space=pltpu.VMEM)
    hbm = pl.BlockSpec(memory_space=pl.ANY)
    return tuple(pl.pallas_call(
        functools.partial(_norm_matmul_piped_kernel, tn=tn),
        in_specs=[vmem, vmem, hbm],
        out_specs=[hbm, hbm],
        out_shape=[jax.ShapeDtypeStruct((m, n), F32), jax.ShapeDtypeStruct(w.shape, BF16)],
        scratch_shapes=[pltpu.VMEM((m, d), BF16)],
        compiler_params=pltpu.CompilerParams(vmem_limit_bytes=VMEM_LIMIT),
        name="norm_matmul_cast",
    )(x, g, w))


def _norm_matmul(x, g, w, *, tm, tn):
    m, d = x.shape
    n = w.shape[1]
    emit_bf16 = w.dtype == F32
    if emit_bf16:
        return _norm_matmul_piped(x, g, w, tn=tn)
    assert m % tm == 0 and n % tn == 0
    wspec = pl.BlockSpec((d, tn), lambda i, j: (0, j))
    out_specs = [pl.BlockSpec((tm, tn), lambda i, j: (i, j))]
    out_shape = [jax.ShapeDtypeStruct((m, n), F32)]
    if emit_bf16:
        out_specs.append(wspec)
        out_shape.append(jax.ShapeDtypeStruct(w.shape, BF16))
    res = pl.pallas_call(
        functools.partial(_norm_matmul_kernel, emit_bf16=emit_bf16),
        grid=(m // tm, n // tn),
        in_specs=[
            pl.BlockSpec((tm, d), lambda i, j: (i, 0)),
            pl.BlockSpec((1, d), lambda i, j: (0, 0)),
            wspec,
        ],
        out_specs=out_specs,
        out_shape=out_shape,
        scratch_shapes=[pltpu.VMEM((tm, d), BF16)],
        compiler_params=_params("parallel", "arbitrary"),
        name="norm_matmul_cast" if emit_bf16 else "norm_matmul",
    )(x, g, w)
    return tuple(res) if emit_bf16 else res[0]


def _mem_kv_kernel(x_ref, g_ref, wk_ref, wv_ref, k_ref, v_ref):
    xn = _rms(x_ref[...], g_ref[...]).astype(BF16)
    k_ref[...] = _dot(xn, wk_ref[...].astype(BF16))
    v_ref[...] = _dot(xn, wv_ref[...].astype(BF16))


def _mem_kv(x, g, wk, wv, *, tm):
    m, d = x.shape
    n = wk.shape[1]
    assert m % tm == 0 and wv.shape == wk.shape
    const = lambda i: (0, 0)
    ospec = pl.BlockSpec((tm, n), lambda i: (i, 0))
    return pl.pallas_call(
        _mem_kv_kernel,
        grid=(m // tm,),
        in_specs=[pl.BlockSpec((tm, d), lambda i: (i, 0)), pl.BlockSpec((1, d), const),
                  pl.BlockSpec((d, n), const), pl.BlockSpec((d, n), const)],
        out_specs=[ospec, ospec],
        out_shape=[jax.ShapeDtypeStruct((m, n), F32)] * 2,
        compiler_params=_params("parallel"),
        name="mem_kv",
    )(x, g, wk, wv)


def _outproj_kernel(x_ref, a_ref, b_ref, wa_ref, wb_ref, o_ref):
    o_ref[...] = x_ref[...] + _dot(a_ref[...], wa_ref[...]) + _dot(b_ref[...], wb_ref[...])


def _outproj(x, a, b, w, *, tm, tn):
    m, d = x.shape
    ka = a.shape[1]
    assert w.shape[0] == 2 * ka and b.shape[1] == ka and m % tm == 0 and d % tn == 0
    return pl.pallas_call(
        _outproj_kernel,
        grid=(m // tm, d // tn),
        in_specs=[
            pl.BlockSpec((tm, tn), lambda i, j: (i, j)),
            pl.BlockSpec((tm, ka), lambda i, j: (i, 0)),
            pl.BlockSpec((tm, ka), lambda i, j: (i, 0)),
            pl.BlockSpec((ka, tn), lambda i, j: (0, j)),
            pl.BlockSpec((ka, tn), lambda i, j: (1, j)),
        ],
        out_specs=pl.BlockSpec((tm, tn), lambda i, j: (i, j)),
        out_shape=jax.ShapeDtypeStruct((m, d), F32),
        compiler_params=_params("parallel", "arbitrary"),
        name="outproj",
    )(x, a, b, w, w)


def _hgrn_structure(bk):
    r = np.arange(ROWS)
    blk = r // bk
    tri = (blk[:, None] == blk[None, :]) & (r[None, :] <= r[:, None])
    masks = [np.eye(ROWS, dtype=bool)]
    m = 1
    while m < bk:
        grp = r // (2 * m)
        is_r = (r % (2 * m)) >= m
        masks.append((grp[:, None] == grp[None, :]) & is_r[:, None] & (~is_r)[None, :])
        m *= 2
    nblk = ROWS // bk
    rsel = blk[:, None] == (np.arange(nblk * LANE) // LANE)[None, :]
    return jnp.asarray(tri, BF16), jnp.asarray(np.stack(masks), F32), jnp.asarray(rsel, F32), nblk


def _hgrn_gates(zf_ref, lb_ref, tri_ref):
    lbr = lb_ref[...]
    e = jnp.exp(lbr - jnp.max(lbr, axis=0, keepdims=True))
    lb = e[0:1] / jnp.sum(e, axis=0, keepdims=True)
    f = lb + (1.0 - lb) * _sigmoid(zf_ref[...])
    parts = _split3(jnp.log2(f))
    tri = tri_ref[...]
    b = _dot(tri, parts[0]) + _dot(tri, parts[1]) + _dot(tri, parts[2])
    return f, parts, b


def _level_signs(row, bk):
    sgns = []
    m = 2
    while m < bk:
        sgns.append(jnp.where((row & (2 * m - 1)) >= m, 1.0, -1.0))
        m *= 2
    return sgns


def _level_weights(f, b, row, sgns, bk):
    width = b.shape[1]
    ws = [jnp.where((row & 1) == 1, f, 1.0)]
    m = 2
    while m < bk:
        g = max(2 * m, 8)
        bg = b.reshape(ROWS // g, g, width)
        if 2 * m >= 8:
            ref = bg[:, m - 1:m, :]
        else:
            sub = row.reshape(ROWS // 8, 8, width) & 7
            ref = jnp.where(sub < 4, bg[:, 1:2, :], bg[:, 5:6, :])
        ws.append(jnp.exp2((bg - ref).reshape(ROWS, width) * sgns[len(ws) - 1]))
        m *= 2
    return ws


def _intra_scores(q, k, ws, mask_ref):
    a = mask_ref[0] * _dot_nt(q.astype(BF16), k.astype(BF16))
    for l, w in enumerate(ws):
        a = a + mask_ref[1 + l] * _dot_nt((q * w).astype(BF16), (k * w).astype(BF16))
    return a.astype(BF16)


def _hgrn_seq_kernel(zq_ref, zf_ref, zi_ref, zg_ref, lb_ref, gn_ref, tri_ref, mask_ref, s0_ref,
                     o_ref, sout_ref, st_scr, *, heads):
    t = pl.program_id(1)

    @pl.when(t == 0)
    def _():
        for h in range(heads):
            st_scr[h] = s0_ref[0, h].T

    row = lax.broadcasted_iota(jnp.int32, (ROWS, LANE), 0)
    sgns = _level_signs(row, ROWS)
    for tt in range(zq_ref.shape[0] // ROWS):
        rs = slice(tt * ROWS, (tt + 1) * ROWS)
        f_all, _, b_all = _hgrn_gates(zf_ref.at[rs, :], lb_ref, tri_ref)
        for h in range(heads):
            sl = slice(h * LANE, (h + 1) * LANE)
            zq = zq_ref[rs, sl]
            q = zq * _sigmoid(zq)
            f = f_all[:, sl]
            k = 1.0 - f
            b = b_all[:, sl]
            a = _intra_scores(q, k, _level_weights(f, b, row, sgns, ROWS), mask_ref)
            v16 = zi_ref[rs, sl].astype(BF16)
            st = st_scr[h]
            bend = b[ROWS - 1:ROWS, :]
            o = _dot(a, v16) + _dot_nt((q * jnp.exp2(b)).astype(BF16), st.astype(BF16))
            kc = (k * jnp.exp2(bend - b)).astype(BF16)
            st_scr[h] = jnp.exp2(bend) * st + _dot_tn(v16, kc)
            zg = zg_ref[rs, sl]
            o_ref[rs, sl] = (_rms(o, gn_ref[...]) * (zg * _sigmoid(zg))).astype(o_ref.dtype)

    @pl.when(t == pl.num_programs(1) - 1)
    def _():
        for h in range(heads):
            sout_ref[0, h] = st_scr[h].T


def _hgrn_blocks_kernel(zq_ref, zf_ref, zi_ref, zg_ref, lb_ref, gn_ref, tri_ref, mask_ref, rsel_ref, s0_ref,
                        o_ref, sout_ref, *, heads, bk, nblk):
    f_all, parts, b_all = _hgrn_gates(zf_ref, lb_ref, tri_ref)
    row = lax.broadcasted_iota(jnp.int32, (ROWS, LANE), 0)
    sgns = _level_signs(row, bk)
    rsel = rsel_ref[...]
    rs16 = rsel.astype(BF16)
    rs16x3 = jnp.concatenate([rs16] * 3, axis=0)
    for h in range(heads):
        sl = slice(h * LANE, (h + 1) * LANE)
        zq = zq_ref[:, sl]
        q = zq * _sigmoid(zq)
        f = f_all[:, sl]
        k = 1.0 - f
        b = b_all[:, sl]
        a = _intra_scores(q, k, _level_weights(f, b, row, sgns, bk), mask_ref)
        v = zi_ref[:, sl]
        scat = jnp.concatenate([s0_ref[i, h] for i in range(nblk)], axis=1)
        y = _dot((q * jnp.exp2(b)).astype(BF16), scat.astype(BF16))
        o = _dot(a, v.astype(BF16)) + jnp.concatenate(
            [y[i * bk:(i + 1) * bk, i * LANE:(i + 1) * LANE] for i in range(nblk)], axis=0)
        bg = b.reshape(nblk, bk, LANE)
        c = (bg[:, bk - 1:bk, :] - bg).reshape(ROWS, LANE)
        kc = (k * jnp.exp2(c)).astype(BF16)
        vblk = (jnp.concatenate([v] * nblk, axis=1) * rsel).astype(BF16)
        bend_t = _dot_tn(jnp.concatenate([p[:, sl] for p in parts], axis=0), rs16x3)
        snew = jnp.exp2(bend_t) * scat + _dot_tn(kc, vblk)
        for i in range(nblk):
            sout_ref[i, h] = snew[:, i * LANE:(i + 1) * LANE]
        zg = zg_ref[:, sl]
        o_ref[:, sl] = (_rms(o, gn_ref[...]) * (zg * _sigmoid(zg))).astype(o_ref.dtype)


def _hgrn(z, lbp, gnorm, s0, *, bk, tiles, hs, tps=1):
    m = z.shape[0]
    nseq, heads = s0.shape[0], s0.shape[1]
    tri, masks, rsel, nblk = _hgrn_structure(bk)
    groups = m // (ROWS * tiles)
    assert groups * nblk == nseq and groups * tiles * ROWS == m and heads % hs == 0 and tiles % tps == 0
    steps = tiles // tps
    hb = heads // hs
    w = hs * LANE
    seq_mode = nblk == 1

    def zspec(part):
        if seq_mode:
            return pl.BlockSpec((tps * ROWS, w), lambda g, j: (g * steps + j, part))
        return pl.BlockSpec((ROWS, w), lambda g, j: (g, part * hb + j))

    const2 = lambda g, j: (0, 0)
    const3 = lambda g, j: (0, 0, 0)
    common = [zspec(0), zspec(1), zspec(2), zspec(3),
              pl.BlockSpec((lbp.shape[0], w), const2 if seq_mode else (lambda g, j: (0, j))),
              pl.BlockSpec((1, LANE), const2),
              pl.BlockSpec(tri.shape, const2),
              pl.BlockSpec(masks.shape, const3)]
    out_shape = [jax.ShapeDtypeStruct((m, heads * LANE), BF16), jax.ShapeDtypeStruct(s0.shape, s0.dtype)]
    if seq_mode:
        assert hs == heads
        sspec = pl.BlockSpec((1, heads, LANE, LANE), lambda g, j: (g, 0, 0, 0))
        return pl.pallas_call(
            functools.partial(_hgrn_seq_kernel, heads=heads),
            grid=(groups, steps),
            in_specs=common + [sspec],
            out_specs=[pl.BlockSpec((tps * ROWS, w), lambda g, j: (g * steps + j, 0)), sspec],
            out_shape=out_shape,
            scratch_shapes=[pltpu.VMEM((heads, LANE, LANE), F32)],
            compiler_params=_params("parallel", "arbitrary"),
            name="hgrn_seq",
        )(z, z, z, z, lbp, gnorm, tri, masks, s0)
    assert tiles == 1
    sspec = pl.BlockSpec((nblk, hs, LANE, LANE), lambda g, j: (g, j, 0, 0))
    return pl.pallas_call(
        functools.partial(_hgrn_blocks_kernel, heads=hs, bk=bk, nblk=nblk),
        grid=(groups, hb),
        in_specs=common + [pl.BlockSpec(rsel.shape, const2), sspec],
        out_specs=[pl.BlockSpec((ROWS, w), lambda g, j: (g, j)), sspec],
        out_shape=out_shape,
        compiler_params=_params("parallel", "parallel"),
        name="hgrn_blocks",
    )(z, z, z, z, lbp, gnorm, tri, masks, rsel, s0)


def _conv_kernel(za_ref, zb_ref, buf_ref, w_ref, cb_ref, lg_ref, lb_ref, o_ref, bufout_ref, ucat, dw,
                 *, ns, ls, nc):
    u = za_ref[...] * _sigmoid(zb_ref[...])

    @pl.when(pl.program_id(1) == 0)
    def _():
        for i in range(ns):
            for cb in range(nc):
                ucat[i, cb, 0:HIST, :] = buf_ref[i, :, cb * LANE:(cb + 1) * LANE]

    for i in range(ns):
        for cb in range(nc):
            ucat[i, cb, HIST:HIST + ls, :] = u[i * ls:(i + 1) * ls, cb * LANE:(cb + 1) * LANE]

    rt = min(ls, 64)

    def cb_body(cb, carry):
        for i in range(ns):
            for r0 in range(0, ls, rt):
                acc = jnp.broadcast_to(cb_ref[cb], (rt, LANE))
                for j in range(CONV_W):
                    acc = acc + ucat[i, cb, r0 + j:r0 + j + rt, :] * w_ref[cb, j:j + 1, :]
                dw[i, cb, r0:r0 + rt, :] = acc
        return carry

    lax.fori_loop(0, nc, cb_body, 0)

    rt2 = min(ls, 32)
    inv_c = 1.0 / (nc * LANE)
    for i in range(ns):
        for r0 in range(0, ls, rt2):
            x = dw[i, :, r0:r0 + rt2, :]
            mu = jnp.sum(jnp.sum(x, axis=0), axis=-1, keepdims=True) * inv_c
            xc = x - mu[None]
            var = jnp.sum(jnp.sum(xc * xc, axis=0), axis=-1, keepdims=True) * inv_c
            y = xc * lax.rsqrt(var + EPS)[None] * lg_ref[...] + lb_ref[...]
            y = y * _sigmoid(y)
            for cb in range(nc):
                o_ref[i * ls + r0:i * ls + r0 + rt2, cb * LANE:(cb + 1) * LANE] = y[cb].astype(o_ref.dtype)

    for i in range(ns):
        for cb in range(nc):
            tail = ucat[i, cb, ls:ls + HIST, :]
            bufout_ref[i, :, cb * LANE:(cb + 1) * LANE] = tail
            ucat[i, cb, 0:HIST, :] = tail


def _conv(z, buf0, w3, cb3, lg3, lb3, *, ns, ls, tiles, za_col):
    m = z.shape[0]
    nseq, _, c = buf0.shape
    nc = c // LANE
    groups = nseq // ns
    assert groups * tiles * ns * ls == m
    rows = ns * ls
    pad_rows = -(-(HIST + ls) // 8) * 8
    bspec = pl.BlockSpec((ns, HIST, c), lambda g, t: (g, 0, 0))
    const3 = lambda g, t: (0, 0, 0)
    return pl.pallas_call(
        functools.partial(_conv_kernel, ns=ns, ls=ls, nc=nc),
        grid=(groups, tiles),
        in_specs=[
            pl.BlockSpec((rows, c), lambda g, t: (g * tiles + t, za_col)),
            pl.BlockSpec((rows, c), lambda g, t: (g * tiles + t, za_col + 1)),
            bspec,
            pl.BlockSpec(w3.shape, const3),
            pl.BlockSpec(cb3.shape, const3),
            pl.BlockSpec(lg3.shape, const3),
            pl.BlockSpec(lb3.shape, const3),
        ],
        out_specs=[pl.BlockSpec((rows, c), lambda g, t: (g * tiles + t, 0)), bspec],
        out_shape=[jax.ShapeDtypeStruct((m, c), BF16), jax.ShapeDtypeStruct(buf0.shape, buf0.dtype)],
        scratch_shapes=[pltpu.VMEM((ns, nc, pad_rows, LANE), F32), pltpu.VMEM((ns, nc, ls, LANE), F32)],
        compiler_params=_params("parallel", "arbitrary"),
        name="conv",
    )(z, z, buf0, w3, cb3, lg3, lb3)


def _xattn_kernel(x_ref, g_ref, wq_ref, k_ref, v_ref, wo_ref, *rest, nb, lq, heads, interleaved):
    o_ref = rest[-1]
    x = x_ref[...]
    if len(rest) > 1:
        a_ref, b_ref, wa_ref, wb_ref = rest[:4]
        x = x + _dot(a_ref[...], wa_ref[...]) + _dot(b_ref[...], wb_ref[...])
    q = _dot(_rms(x, g_ref[...]).astype(BF16), wq_ref[...])
    hd = q.shape[1] // heads
    kv_rows = k_ref.shape[0] // heads if interleaved else k_ref.shape[0]
    mem = kv_rows // nb
    scale = hd ** -0.5
    if nb > 1:
        rows = lax.broadcasted_iota(jnp.int32, (nb * lq, nb * mem), 0) // lq
        cols = lax.broadcasted_iota(jnp.int32, (nb * lq, nb * mem), 1) // mem
        own = rows == cols
    outs = []
    for h in range(heads):
        sl = slice(h * hd, (h + 1) * hd)
        if interleaved:
            kh = k_ref[pl.ds(h, kv_rows, stride=heads), :].astype(BF16)
            vh = v_ref[pl.ds(h, kv_rows, stride=heads), :].astype(BF16)
        else:
            kh = k_ref[:, sl].astype(BF16)
            vh = v_ref[:, sl].astype(BF16)
        s = _dot_nt(q[:, sl].astype(BF16), kh) * scale
        if nb > 1:
            s = jnp.where(own, s, -1e30)
        p = jnp.exp(s - jnp.max(s, axis=-1, keepdims=True))
        l = jnp.sum(p, axis=-1, keepdims=True)
        outs.append((_dot(p.astype(BF16), vh) / l).astype(BF16))
    o_ref[...] = x + _dot(jnp.concatenate(outs, axis=1), wo_ref[...])


def _xattn(x, g, wq, k2d, v2d, wo, *, nb, lq, mem, heads, mix=None):
    m, d = x.shape
    xa = wq.shape[1]
    interleaved = k2d.shape[1] != xa
    kv_blk = (nb * mem * heads, xa // heads) if interleaved else (nb * mem, xa)
    rows = nb * lq
    assert m % rows == 0
    steps = m // rows
    nseq = k2d.shape[0] * nb // kv_blk[0]
    tiles = steps * nb // nseq
    assert tiles * nseq == steps * nb and (nb == 1 or tiles == 1)
    kv_map = lambda i: (i // tiles, 0)
    const = lambda i: (0, 0)
    in_specs = [
        pl.BlockSpec((rows, d), lambda i: (i, 0)),
        pl.BlockSpec((1, d), const),
        pl.BlockSpec((d, xa), const),
        pl.BlockSpec(kv_blk, kv_map),
        pl.BlockSpec(kv_blk, kv_map),
        pl.BlockSpec((xa, d), const),
    ]
    args = [x, g, wq, k2d, v2d, wo]
    if mix is not None:
        a, b, w = mix
        ka = a.shape[1]
        assert w.shape == (2 * ka, d) and b.shape == a.shape
        in_specs += [pl.BlockSpec((rows, ka), lambda i: (i, 0)), pl.BlockSpec((rows, ka), lambda i: (i, 0)),
                     pl.BlockSpec((ka, d), const), pl.BlockSpec((ka, d), lambda i: (1, 0))]
        args += [a, b, w, w]
    return pl.pallas_call(
        functools.partial(_xattn_kernel, nb=nb, lq=lq, heads=heads, interleaved=interleaved),
        grid=(steps,),
        in_specs=in_specs,
        out_specs=pl.BlockSpec((rows, d), lambda i: (i, 0)),
        out_shape=jax.ShapeDtypeStruct((m, d), F32),
        compiler_params=_params("parallel"),
        name="xattn" if mix is None else "outproj_xattn",
    )(*args)


def kernel(x_prompt, x_sample, mem_prompt, state_hgrn, state_conv, cache_mem_k, cache_mem_v, norm_ffn1, ffn1_w_gate, ffn1_w_up, ffn1_w_down, norm_mix, w_in, hgrn_lb, hgrn_gnorm, conv_w, conv_b, conv_ln_g, conv_ln_b, w_out, norm_xattn, norm_mem, xattn_wq, xattn_wk, xattn_wv, xattn_wo, norm_ffn2, ffn2_w_gate, ffn2_w_up, ffn2_w_down, norm_final):
    depth = norm_ffn1.shape[0]
    assert depth == 1
    nb_p, seq, d = x_prompt.shape
    nb_s, dseq, _ = x_sample.shape
    mem = mem_prompt.shape[1]
    heads_a, expand, vdim = state_hgrn.shape[2:]
    c_conv = state_conv.shape[-1]
    xa_heads, xa_hd = cache_mem_k.shape[3:]
    xa = xa_heads * xa_hd
    assert expand == LANE and vdim == LANE and c_conv == heads_a * LANE
    za_col = (2 * heads_a * expand + 2 * heads_a * vdim) // c_conv

    l = 0
    row = lambda p: p.reshape(1, -1).astype(F32)
    bf = lambda w: w[l].astype(BF16)
    w_out16 = bf(w_out)
    wq16, wo16 = bf(xattn_wq), bf(xattn_wo)
    n_mix, n_xa, n_mem, n_fin = row(norm_mix[l]), row(norm_xattn[l]), row(norm_mem[l]), row(norm_final)
    lbp = hgrn_lb.astype(F32)
    gnorm = row(hgrn_gnorm[l])
    nc = c_conv // LANE
    chan3 = lambda p: p.reshape(-1, nc, LANE).transpose(1, 0, 2).astype(F32)
    w3, cb3, lg3, lb3 = chan3(conv_w[l]), chan3(conv_b[l]), chan3(conv_ln_g[l]), chan3(conv_ln_b[l])
    ones_d = jnp.ones((1, d), F32)

    def block(x, k2d, v2d, s0, buf0, f1w, w_in_x, f2w, *, hg, cv, xat, tm, tf, tn, fuse_out):
        x = _ffn(x, row(norm_ffn1[l]), *f1w, ones_d, final_norm=False, tf=tf)
        if f1w[0].dtype == F32:
            x, f1w = x
        z = _norm_matmul(x, n_mix, w_in_x, tm=tm, tn=tn)
        if w_in_x.dtype == F32:
            z, w_in_x = z
        o_a, s_new = _hgrn(z, lbp, gnorm, s0, **hg)
        o_b, buf_new = _conv(z, buf0, w3, cb3, lg3, lb3, za_col=za_col, **cv)
        if fuse_out:
            x = _xattn(x, n_xa, wq16, k2d, v2d, wo16, mem=mem, heads=xa_heads, mix=(o_a, o_b, w_out16), **xat)
        else:
            x = _outproj(x, o_a, o_b, w_out16, tm=512, tn=d)
            x = _xattn(x, n_xa, wq16, k2d, v2d, wo16, mem=mem, heads=xa_heads, **xat)
        y = _ffn(x, row(norm_ffn2[l]), *f2w, n_fin, final_norm=True, tf=tf)
        if f2w[0].dtype == F32:
            y, f2w = y
        return y, s_new, buf_new, (f1w, w_in_x, f2w)

    ys, shs, scs, w16 = block(
        x_sample.reshape(nb_s * dseq, d),
        cache_mem_k[l].reshape(nb_s * mem * xa_heads, xa_hd),
        cache_mem_v[l].reshape(nb_s * mem * xa_heads, xa_hd),
        state_hgrn[l], state_conv[l],
        (ffn1_w_gate[l], ffn1_w_up[l], ffn1_w_down[l]), w_in[l],
        (ffn2_w_gate[l], ffn2_w_up[l], ffn2_w_down[l]),
        hg=dict(bk=dseq, tiles=1, hs=4),
        cv=dict(ns=8, ls=dseq, tiles=1),
        xat=dict(nb=8, lq=dseq), tm=1024, tf=256, tn=512, fuse_out=False)

    memx = mem_prompt.reshape(nb_p * mem, d)
    mk, mv = _mem_kv(memx, n_mem, xattn_wk[l], xattn_wv[l], tm=256)
    s0p = jnp.zeros((nb_p, heads_a, expand, vdim), state_hgrn.dtype)
    b0p = jnp.zeros((nb_p, HIST, c_conv), state_conv.dtype)
    yp, shp, scp, _ = block(
        x_prompt.reshape(nb_p * seq, d), mk, mv, s0p, b0p, *w16,
        hg=dict(bk=ROWS, tiles=seq // ROWS, hs=heads_a, tps=4),
        cv=dict(ns=1, ls=256, tiles=seq // 256),
        xat=dict(nb=1, lq=512), tm=1024, tf=512, tn=1536, fuse_out=True)

    return (yp.reshape(nb_p, seq, d), ys.reshape(nb_s, dseq, d),
            shp[None], scp[None],
            mk.reshape(1, nb_p, mem, xa_heads, xa_hd).astype(cache_mem_k.dtype),
            mv.reshape(1, nb_p, mem, xa_heads, xa_hd).astype(cache_mem_v.dtype),
            shs[None], scs[None])
```
